```python
import jax, jax.numpy as jnp
from jax import lax
import numpy as np

D_MODEL = 1024
BATCH = 8
SEQ = 2048
DEPTH = 2
DEC_BATCH = 128
DEC_SEQ = 4
PAST_LEN = 8192
PAGE_SIZE = 128

C_CONV = D_MODEL // 2
CONV_WIDTH = 31
H_FOX = 8
DH_FOX = D_MODEL // 16
H_MLA = 8
MLA_NOPE = D_MODEL // 16
MLA_ROPE = D_MODEL // 32
MLA_V = D_MODEL // 16
KV_LORA = D_MODEL // 4
Q_LORA = 3 * KV_LORA
D_FF = 4 * D_MODEL
N_BRANCH = 3
Q_BLOCK = 128
ROPE_BASE = 10000.0
FORGET_BIAS = 4.0
NEG_INF = -1e30
DEEPNORM_ALPHA = (2 * DEPTH) ** 0.25
DEEPNORM_BETA = (8 * DEPTH) ** -0.25
IN_SPLIT_SIZES = (C_CONV, C_CONV, H_FOX * DH_FOX, H_FOX * DH_FOX, H_FOX * DH_FOX, H_FOX,
                  Q_LORA, KV_LORA, MLA_ROPE, N_BRANCH * D_MODEL)
IN_SPLIT_POINTS = tuple(sum(IN_SPLIT_SIZES[:i + 1]) for i in range(len(IN_SPLIT_SIZES) - 1))
N_IN = sum(IN_SPLIT_SIZES)

kernel_name = 'hybrid_conv_fox_mla_adaln_decode_step'


def layer_norm(x, gain=None, bias=None, eps=1e-5):
    xf = x.astype(jnp.float32)
    mu = jnp.mean(xf, -1, keepdims=True)
    var = jnp.mean(jnp.square(xf - mu), -1, keepdims=True)
    y = (xf - mu) * lax.rsqrt(var + eps)
    if gain is not None:
        y = y * gain.astype(jnp.float32) + bias.astype(jnp.float32)
    return y.astype(x.dtype)


def rms_norm(x, gain, eps=1e-6):
    xf = x.astype(jnp.float32)
    y = xf * lax.rsqrt(jnp.mean(jnp.square(xf), -1, keepdims=True) + eps)
    return (y * gain.astype(jnp.float32)).astype(x.dtype)


def rope(x, pos):
    half = x.shape[-1] // 2
    inv_freq = ROPE_BASE ** (-jnp.arange(half, dtype=jnp.float32) / half)
    ang = pos.astype(jnp.float32)[:, None] * inv_freq
    ang = ang.reshape(ang.shape[0], *([1] * (x.ndim - 3)), half)
    cos, sin = jnp.cos(ang), jnp.sin(ang)
    xf = x.astype(jnp.float32)
    x1, x2 = xf[..., :half], xf[..., half:]
    return jnp.concatenate([x1 * cos - x2 * sin, x2 * cos + x1 * sin], -1).astype(x.dtype)


def gather_pages(pool, layer, page_table):
    g = pool[layer, page_table]
    return g.reshape(g.shape[0], g.shape[1] * g.shape[2], *g.shape[3:])


def sweep_query_blocks(attend, q_arrays, q_pos):
    n_blk = q_pos.shape[0] // Q_BLOCK
    def to_blocks(a):
        return jnp.moveaxis(a.reshape(a.shape[0], n_blk, Q_BLOCK, *a.shape[2:]), 1, 0)
    xs = tuple(to_blocks(a) for a in q_arrays) + (q_pos.reshape(n_blk, Q_BLOCK),)
    out = lax.map(lambda blk: attend(*blk), xs)
    out = jnp.moveaxis(out, 0, 1)
    return out.reshape(out.shape[0], n_blk * Q_BLOCK, *out.shape[3:])


def fox_attend(q, q_off, q_pos, k, v, k_off, k_pos):
    s = jnp.einsum('bqhd,bkhd->bhqk', q, k).astype(jnp.float32) * (DH_FOX ** -0.5)
    s = s + jnp.moveaxis(q_off, 1, 2)[..., :, None] - jnp.moveaxis(k_off, 1, 2)[..., None, :]
    s = jnp.where(k_pos[None, :] <= q_pos[:, None], s, NEG_INF)
    p = jax.nn.softmax(s, axis=-1).astype(v.dtype)
    return jnp.einsum('bhqk,bkhd->bqhd', p, v)


def mla_attend(q_abs, q_rope, q_pos, lat, k_rope, k_pos):
    s = (jnp.einsum('bqhc,bkc->bhqk', q_abs, lat) + jnp.einsum('bqhr,bkr->bhqk', q_rope, k_rope)).astype(jnp.float32)
    s = s * ((MLA_NOPE + MLA_ROPE) ** -0.5)
    s = jnp.where(k_pos[None, :] <= q_pos[:, None], s, NEG_INF)
    p = jax.nn.softmax(s, axis=-1).astype(lat.dtype)
    return jnp.einsum('bhqk,bkc->bqhc', p, lat)


def token_mixers(h, pos, p, past):
    B, T, _ = h.shape
    (u_a, u_b, q_f, k_f, v_f, f_logit, q_lat, kv_lat, k_rope_raw, gate_logit) = jnp.split(
        h @ p['w_in'], IN_SPLIT_POINTS, axis=-1)

    u = u_a * jax.nn.sigmoid(u_b)
    buf = jnp.zeros((B, CONV_WIDTH - 1, C_CONV), h.dtype) if past is None else past['conv']
    u_full = jnp.concatenate([buf, u], axis=1)
    new_conv = u_full[:, -(CONV_WIDTH - 1):]
    y = lax.conv_general_dilated(u_full, p['conv_w'][:, None, :], (1,), 'VALID',
                                 dimension_numbers=('NWC', 'WIO', 'NWC'),
                                 feature_group_count=C_CONV) + p['conv_b']
    y_conv = jax.nn.silu(layer_norm(y, p['conv_ln_g'], p['conv_ln_b'])) @ p['conv_w_out']

    q_f = q_f.reshape(B, T, H_FOX, DH_FOX)
    k_f = k_f.reshape(B, T, H_FOX, DH_FOX)
    v_f = v_f.reshape(B, T, H_FOX, DH_FOX)
    log_f = jax.nn.log_sigmoid((f_logit + p['b_f']).astype(jnp.float32))
    c_new = jnp.cumsum(log_f, axis=1)
    if past is None:
        o_fox = sweep_query_blocks(
            lambda qb, qo, qp: fox_attend(qb, qo, qp, k_f, v_f, c_new, pos), (q_f, c_new), pos)
    else:
        past_logf = past['fox_logf'].astype(jnp.float32)
        suffix = lax.cumsum(past_logf, axis=1, reverse=True) - past_logf
        n_past = past_logf.shape[1]
        k_all = jnp.concatenate([past['fox_k'], k_f], axis=1)
        v_all = jnp.concatenate([past['fox_v'], v_f], axis=1)
        k_off = jnp.concatenate([-suffix, c_new], axis=1)
        k_pos = jnp.concatenate([jnp.arange(n_past, dtype=jnp.int32), pos])
        o_fox = fox_attend(q_f, c_new, pos, k_all, v_all, k_off, k_pos)
    y_fox = o_fox.reshape(B, T, H_FOX * DH_FOX) @ p['fox_w_o']

    q = (rms_norm(q_lat, p['mla_q_norm_g']) @ p['mla_w_qb']).reshape(B, T, H_MLA, MLA_NOPE + MLA_ROPE)
    q_rope = rope(q[..., MLA_NOPE:], pos)
    q_abs = jnp.einsum('bqhn,chn->bqhc', q[..., :MLA_NOPE], p['mla_w_uk'])
    lat = rms_norm(kv_lat, p['mla_kv_norm_g'])
    k_rope = rope(k_rope_raw, pos)
    if past is None:
        o_lat = sweep_query_blocks(
            lambda qa, qr, qp: mla_attend(qa, qr, qp, lat, k_rope, pos), (q_abs, q_rope), pos)
    else:
        n_past = past['mla_latent'].shape[1]
        lat_all = jnp.concatenate([past['mla_latent'], lat], axis=1)
        kr_all = jnp.concatenate([past['mla_krope'], k_rope], axis=1)
        k_pos = jnp.concatenate([jnp.arange(n_past, dtype=jnp.int32), pos])
        o_lat = mla_attend(q_abs, q_rope, pos, lat_all, kr_all, k_pos)
    o_mla = jnp.einsum('bqhc,chv->bqhv', o_lat, p['mla_w_uv']).reshape(B, T, H_MLA * MLA_V)
    y_mla = o_mla @ p['mla_w_o']

    g = jax.nn.sigmoid(gate_logit.reshape(B, T, N_BRANCH, D_MODEL))
    merged = g[:, :, 0] * y_conv + g[:, :, 1] * y_fox + g[:, :, 2] * y_mla
    return merged @ p['w_out'], (k_f, v_f, log_f, lat, k_rope, new_conv)


def trunk_layer(x, c, pos, p, past):
    mod = jax.nn.silu(c) @ p['w_ada'] + p['b_ada']
    shift1, scale1, gate1, shift2, scale2, gate2 = jnp.split(mod[:, None, :], 6, axis=-1)
    h = layer_norm(x) * (1 + scale1) + shift1
    mix, new_rows = token_mixers(h, pos, p, past)
    x = layer_norm(DEEPNORM_ALPHA * x + gate1 * mix, p['ln1_g'], p['ln1_b'])
    h2 = layer_norm(x) * (1 + scale2) + shift2
    ff = jnp.square(jax.nn.relu(h2 @ p['w_up'])) @ p['w_down']
    x = layer_norm(DEEPNORM_ALPHA * x + gate2 * ff, p['ln2_g'], p['ln2_b'])
    return x, new_rows


def setup_inputs(seed: int = 0) -> dict:
    key = jax.random.key(seed)
    ks = iter(jax.random.split(key, 48))
    def nrm(shape, scale=1.0):
        return jax.random.normal(next(ks), shape, jnp.float32) * scale
    n_pages = PAST_LEN // PAGE_SIZE
    n_used = DEC_BATCH * n_pages
    n_pool = n_used + n_used // 4
    L = DEPTH
    inp = {}
    inp['x_prompt'] = nrm((BATCH, SEQ, D_MODEL))
    inp['x_sample'] = nrm((DEC_BATCH, DEC_SEQ, D_MODEL))
    inp['cache_fox_k'] = nrm((L, n_pool, PAGE_SIZE, H_FOX, DH_FOX))
    inp['cache_fox_v'] = nrm((L, n_pool, PAGE_SIZE, H_FOX, DH_FOX))
    inp['cache_fox_logf'] = jax.nn.log_sigmoid(FORGET_BIAS + nrm((L, n_pool, PAGE_SIZE, H_FOX)))
    inp['cache_mla_latent'] = nrm((L, n_pool, PAGE_SIZE, KV_LORA))
    inp['cache_mla_krope'] = nrm((L, n_pool, PAGE_SIZE, MLA_ROPE))
    inp['state_conv'] = nrm((L, DEC_BATCH, CONV_WIDTH - 1, C_CONV), 0.5)
    inp['page_table'] = jax.random.permutation(next(ks), n_pool)[:n_used].reshape(DEC_BATCH, n_pages).astype(jnp.int32)
    inp['c_prompt'] = nrm((BATCH, D_MODEL))
    inp['c_sample'] = nrm((DEC_BATCH, D_MODEL))
    inp['w_ada'] = nrm((L, D_MODEL, 6 * D_MODEL), D_MODEL ** -0.5)
    inp['b_ada'] = nrm((L, 6 * D_MODEL), 0.02)
    inp['w_in'] = nrm((L, D_MODEL, N_IN), D_MODEL ** -0.5)
    inp['b_f'] = FORGET_BIAS + nrm((L, H_FOX), 0.1)
    inp['conv_w'] = nrm((L, CONV_WIDTH, C_CONV), CONV_WIDTH ** -0.5)
    inp['conv_b'] = nrm((L, C_CONV), 0.02)
    inp['conv_ln_g'] = 1.0 + nrm((L, C_CONV), 0.02)
    inp['conv_ln_b'] = nrm((L, C_CONV), 0.02)
    inp['conv_w_out'] = nrm((L, C_CONV, D_MODEL), C_CONV ** -0.5)
    inp['fox_w_o'] = nrm((L, H_FOX * DH_FOX, D_MODEL), (H_FOX * DH_FOX) ** -0.5)
    inp['mla_q_norm_g'] = 1.0 + nrm((L, Q_LORA), 0.02)
    inp['mla_w_qb'] = nrm((L, Q_LORA, H_MLA * (MLA_NOPE + MLA_ROPE)), Q_LORA ** -0.5)
    inp['mla_kv_norm_g'] = 1.0 + nrm((L, KV_LORA), 0.02)
    inp['mla_w_uk'] = nrm((L, KV_LORA, H_MLA, MLA_NOPE), KV_LORA ** -0.5)
    inp['mla_w_uv'] = nrm((L, KV_LORA, H_MLA, MLA_V), KV_LORA ** -0.5)
    inp['mla_w_o'] = nrm((L, H_MLA * MLA_V, D_MODEL), (H_MLA * MLA_V) ** -0.5)
    inp['w_out'] = nrm((L, D_MODEL, D_MODEL), DEEPNORM_BETA * D_MODEL ** -0.5)
    inp['ln1_g'] = 1.0 + nrm((L, D_MODEL), 0.02)
    inp['ln1_b'] = nrm((L, D_MODEL), 0.02)
    inp['w_up'] = nrm((L, D_MODEL, D_FF), D_MODEL ** -0.5)
    inp['w_down'] = nrm((L, D_FF, D_MODEL), DEEPNORM_BETA * D_FF ** -0.5)
    inp['ln2_g'] = 1.0 + nrm((L, D_MODEL), 0.02)
    inp['ln2_b'] = nrm((L, D_MODEL), 0.02)
    return inp


def reference(x_prompt, x_sample, cache_fox_k, cache_fox_v, cache_fox_logf, cache_mla_latent, cache_mla_krope,
              state_conv, page_table, c_prompt, c_sample, w_ada, b_ada, w_in, b_f, conv_w, conv_b, conv_ln_g,
              conv_ln_b, conv_w_out, fox_w_o, mla_q_norm_g, mla_w_qb, mla_kv_norm_g, mla_w_uk, mla_w_uv, mla_w_o,
              w_out, ln1_g, ln1_b, w_up, w_down, ln2_g, ln2_b):
    n_past = page_table.shape[1] * cache_fox_k.shape[2]
    pos_prompt = jnp.arange(x_prompt.shape[1], dtype=jnp.int32)
    pos_sample = n_past + jnp.arange(x_sample.shape[1], dtype=jnp.int32)
    xp, xs = x_prompt, x_sample
    rows_p, rows_s = [], []
    for l in range(DEPTH):
        p = {'w_ada': w_ada[l], 'b_ada': b_ada[l], 'w_in': w_in[l], 'b_f': b_f[l], 'conv_w': conv_w[l],
             'conv_b': conv_b[l], 'conv_ln_g': conv_ln_g[l], 'conv_ln_b': conv_ln_b[l], 'conv_w_out': conv_w_out[l],
             'fox_w_o': fox_w_o[l], 'mla_q_norm_g': mla_q_norm_g[l], 'mla_w_qb': mla_w_qb[l],
             'mla_kv_norm_g': mla_kv_norm_g[l], 'mla_w_uk': mla_w_uk[l], 'mla_w_uv': mla_w_uv[l],
             'mla_w_o': mla_w_o[l], 'w_out': w_out[l], 'ln1_g': ln1_g[l], 'ln1_b': ln1_b[l], 'w_up': w_up[l],
             'w_down': w_down[l], 'ln2_g': ln2_g[l], 'ln2_b': ln2_b[l]}
        past = {'fox_k': gather_pages(cache_fox_k, l, page_table),
                'fox_v': gather_pages(cache_fox_v, l, page_table),
                'fox_logf': gather_pages(cache_fox_logf, l, page_table),
                'mla_latent': gather_pages(cache_mla_latent, l, page_table),
                'mla_krope': gather_pages(cache_mla_krope, l, page_table),
                'conv': state_conv[l]}
        xp, r_p = trunk_layer(xp, c_prompt, pos_prompt, p, None)
        xs, r_s = trunk_layer(xs, c_sample, pos_sample, p, past)
        rows_p.append(r_p)
        rows_s.append(r_s)
    def stk(rows, i):
        return jnp.stack([r[i] for r in rows])
    return (xp, xs,
            stk(rows_p, 0), stk(rows_p, 1), stk(rows_p, 2), stk(rows_p, 3), stk(rows_p, 4), stk(rows_p, 5),
            stk(rows_s, 0), stk(rows_s, 1), stk(rows_s, 2), stk(rows_s, 3), stk(rows_s, 4), stk(rows_s, 5))
```

```python
import functools
import math

import jax
import jax.numpy as jnp
from jax import lax
from jax.experimental import pallas as pl
from jax.experimental.pallas import tpu as pltpu

F32 = jnp.float32
BF16 = jnp.bfloat16

D_MODEL = 1024
C_CONV = D_MODEL // 2
CONV_WIDTH = 31
N_HEADS = 8
DH = D_MODEL // 16
ROPE = D_MODEL // 32
KV_LORA = D_MODEL // 4
Q_LORA = 3 * KV_LORA
D_FF = 4 * D_MODEL
ROPE_BASE = 10000.0
NEG_INF = -1e30
LANES = 128
HEAD_PAD = 128
VMEM_LIMIT = 56 * 1024 * 1024

_NT = (((1,), (1,)), ((), ()))


def _params(sem, vmem=VMEM_LIMIT):
    return pltpu.CompilerParams(dimension_semantics=sem, vmem_limit_bytes=vmem)


def _const_spec(shape):
    nd = len(shape)
    return pl.BlockSpec(shape, lambda *_: (0,) * nd, pipeline_mode=pl.Buffered(1))


def _ln(x, eps=1e-5):
    mu = jnp.mean(x, axis=-1, keepdims=True)
    xc = x - mu
    var = jnp.mean(xc * xc, axis=-1, keepdims=True)
    return xc * lax.rsqrt(var + eps)


def _rms(x, eps=1e-6):
    return x * lax.rsqrt(jnp.mean(x * x, axis=-1, keepdims=True) + eps)


def _sigmoid(x):
    return 1.0 / (1.0 + jnp.exp(-x))


def _log_sigmoid(x):
    return jnp.minimum(x, 0.0) - jnp.log(1.0 + jnp.exp(-jnp.abs(x)))


def _mm(a, b):
    return jnp.dot(a, b, preferred_element_type=F32)


def _ada_kernel(c_ref, w_ref, b_ref, o_ref):
    c = c_ref[...]
    a = (c * _sigmoid(c)).astype(BF16)
    o_ref[...] = _mm(a, w_ref[...].astype(BF16)) + b_ref[...]


def _ada(c_all, w_ada, b_ada):
    n_layers, _, n_out = w_ada.shape
    m = c_all.shape[0]
    tn = 1536
    return pl.pallas_call(
        _ada_kernel,
        grid=(n_layers, n_out // tn),
        in_specs=[pl.BlockSpec((m, D_MODEL), lambda l, j: (0, 0)),
                  pl.BlockSpec((None, D_MODEL, tn), lambda l, j: (l, 0, j)),
                  pl.BlockSpec((None, 1, tn), lambda l, j: (l, 0, j))],
        out_specs=pl.BlockSpec((None, m, tn), lambda l, j: (l, 0, j)),
        out_shape=jax.ShapeDtypeStruct((n_layers, m, n_out), F32),
        compiler_params=_params(("parallel", "parallel")),
        name="ada_mod",
    )(c_all, w_ada, b_ada.reshape(n_layers, 1, n_out))


def _seg_cumsum(x, seg):
    rows = x.shape[0]
    row = lax.broadcasted_iota(jnp.int32, x.shape, 0)
    rin = row % seg if seg < rows else row
    d = 1
    while d < min(seg, rows):
        x = x + jnp.where(rin >= d, pltpu.roll(x, d, axis=0), 0.0)
        d *= 2
    return x


def _inproj_kernel(*refs, prompt, seg, tiles_per_seq, mla_scale):
    (x_ref, shift_ref, scale_ref, wmain_ref, wql_ref, wkv_ref, ws_ref, bf_ref, gq_ref, gkv_ref,
     cos32_ref, sin32_ref) = refs[:12]
    refs = refs[12:]
    if prompt:
        (wkr_ref, wkrr_ref, cos128_ref, sin128_ref, wqa_ref, wqb_ref, wka_ref, wuv_ref) = refs[:8]
        refs = refs[8:]
        (u_ref, kf_ref, vf_ref, logf_ref, c_ref, lat_ref, kr_ref,
         qf_ref, kb_ref, vb_ref, mq_ref, mk_ref, mv_ref, carry_ref) = refs
    else:
        (wqn_ref, bduk_ref, wqr_ref, wqrr_ref, cos256_ref, sin256_ref) = refs[:6]
        refs = refs[6:]
        (u_ref, kf_ref, vf_ref, logf_ref, c_ref, lat_ref, kr_ref,
         qf_ref, qabs_ref, qrope_ref, carry_ref) = refs

    i = pl.program_id(0)
    x = x_ref[...]
    tm = x.shape[0]
    h = (_ln(x) * (1.0 + scale_ref[...]) + shift_ref[...]).astype(BF16)

    u_ref[...] = _mm(h, wmain_ref[:, 0:C_CONV]) * _sigmoid(_mm(h, wmain_ref[:, C_CONV:2 * C_CONV]))
    o = 2 * C_CONV
    hd = N_HEADS * DH
    qf_ref[...] = (_mm(h, wmain_ref[:, o:o + hd]) * (DH ** -0.5)).astype(qf_ref.dtype)
    kf = _mm(h, wmain_ref[:, o + hd:o + 2 * hd])
    vf = _mm(h, wmain_ref[:, o + 2 * hd:o + 3 * hd])
    kf_ref[...] = kf
    vf_ref[...] = vf
    if prompt:
        kb_ref[...] = kf.astype(BF16)
        vb_ref[...] = vf.astype(BF16)

    zs = _mm(h, ws_ref[...])
    kr_ref[...] = zs[:, 0:ROPE] * cos32_ref[...] + zs[:, ROPE:2 * ROPE] * sin32_ref[...]
    logf = _log_sigmoid(zs + bf_ref[...])
    logf_ref[...] = logf[:, 2 * ROPE:2 * ROPE + N_HEADS]
    cs = _seg_cumsum(logf, seg)
    if seg >= tm:
        @pl.when(i % tiles_per_seq == 0)
        def _():
            carry_ref[...] = jnp.zeros_like(carry_ref)
        cs = cs + carry_ref[...]
        carry_ref[...] = cs[tm - 1:tm, :]
    c_ref[...] = cs[:, 2 * ROPE:2 * ROPE + N_HEADS]

    latn = _rms(_mm(h, wkv_ref[...])) * gkv_ref[...]
    lat_ref[...] = latn
    qn = (_rms(_mm(h, wql_ref[...])) * gq_ref[...]).astype(BF16)
    if prompt:
        latb = latn.astype(BF16)
        cosq = cos128_ref[...]
        sinq = sin128_ref[...]
        krt = _mm(h, wkr_ref[...]) * cosq + _mm(h, wkrr_ref[...]) * sinq
        for hh in range(N_HEADS):
            sl = slice(hh * HEAD_PAD, (hh + 1) * HEAD_PAD)
            za = _mm(qn, wqa_ref[:, sl])
            zb = _mm(qn, wqb_ref[:, sl])
            mq_ref[:, sl] = ((za * cosq + zb * sinq) * mla_scale).astype(BF16)
            mk_ref[:, sl] = (_mm(latb, wka_ref[:, sl]) + krt).astype(BF16)
        mv_ref[...] = _mm(latb, wuv_ref[...]).astype(BF16)
    else:
        qnope = _mm(qn, wqn_ref[...]).astype(BF16)
        qabs_ref[...] = _mm(qnope, bduk_ref[...]) * mla_scale
        qrope_ref[...] = (_mm(qn, wqr_ref[...]) * cos256_ref[...]
                          + _mm(qn, wqrr_ref[...]) * sin256_ref[...]) * mla_scale


def _inproj(x, shift_spec_arr, mod_specs, w, tabs, *, prompt, seg, tm):
    rows = x.shape[0]
    n_tiles = rows // tm
    n_tab = tabs["cos32"].shape[0] // tm
    tiles_per_seq = max(seg // tm, 1)

    def row_spec(width):
        return pl.BlockSpec((tm, width), lambda i: (i, 0))

    def tab_spec(width):
        return pl.BlockSpec((tm, width), lambda i: (i % n_tab, 0))

    common_w = [w["w_main"], w["w_ql"], w["w_kv"], w["w_s"], w["b_f"], w["g_q"], w["g_kv"]]
    in_arrays = [x, shift_spec_arr, shift_spec_arr] + common_w + [tabs["cos32"], tabs["sin32"]]
    in_specs = ([row_spec(D_MODEL), mod_specs[0], mod_specs[1]]
                + [_const_spec(a.shape) for a in common_w] + [tab_spec(ROPE), tab_spec(ROPE)])
    hd = N_HEADS * DH
    out_shapes = [jax.ShapeDtypeStruct((rows, C_CONV), F32),
                  jax.ShapeDtypeStruct((rows, hd), F32),
                  jax.ShapeDtypeStruct((rows, hd), F32),
                  jax.ShapeDtypeStruct((rows, N_HEADS), F32),
                  jax.ShapeDtypeStruct((rows, N_HEADS), F32),
                  jax.ShapeDtypeStruct((rows, KV_LORA), F32),
                  jax.ShapeDtypeStruct((rows, ROPE), F32)]
    out_specs = [row_spec(C_CONV), row_spec(hd), row_spec(hd), row_spec(N_HEADS), row_spec(N_HEADS),
                 row_spec(KV_LORA), row_spec(ROPE)]
    if prompt:
        extra_w = [w["w_kr"], w["w_krr"]]
        in_arrays += extra_w + [tabs["cos128"], tabs["sin128"]]
        in_specs += [_const_spec(a.shape) for a in extra_w] + [tab_spec(LANES), tab_spec(LANES)]
        extra_w2 = [w["w_qa"], w["w_qb"], w["w_ka"], w["w_uv"]]
        in_arrays += extra_w2
        in_specs += [_const_spec(a.shape) for a in extra_w2]
        wide = N_HEADS * HEAD_PAD
        out_shapes += [jax.ShapeDtypeStruct((rows, hd), BF16)] * 3 + [
            jax.ShapeDtypeStruct((rows, wide), BF16), jax.ShapeDtypeStruct((rows, wide), BF16),
            jax.ShapeDtypeStruct((rows, hd), BF16)]
        out_specs += [row_spec(hd)] * 3 + [row_spec(wide), row_spec(wide), row_spec(hd)]
    else:
        extra_w = [w["w_qn"], w["bd_uk"], w["w_qr"], w["w_qrr"]]
        in_arrays += extra_w + [tabs["cos256"], tabs["sin256"]]
        in_specs += [_const_spec(a.shape) for a in extra_w] + [tab_spec(N_HEADS * ROPE)] * 2
        out_shapes += [jax.ShapeDtypeStruct((rows, hd), F32),
                       jax.ShapeDtypeStruct((rows, N_HEADS * KV_LORA), F32),
                       jax.ShapeDtypeStruct((rows, N_HEADS * ROPE), F32)]
        out_specs += [row_spec(hd), row_spec(N_HEADS * KV_LORA), row_spec(N_HEADS * ROPE)]

    kern = functools.partial(_inproj_kernel, prompt=prompt, seg=seg, tiles_per_seq=tiles_per_seq,
                             mla_scale=float((DH + ROPE) ** -0.5))
    return pl.pallas_call(
        kern, grid=(n_tiles,), in_specs=in_specs, out_specs=out_specs, out_shape=out_shapes,
        scratch_shapes=[pltpu.VMEM((1, LANES), F32)],
        compiler_params=_params(("arbitrary",)),
        name="inproj_prompt" if prompt else "inproj_sample",
    )(*in_arrays)


_HALO = 32


def _conv_prompt_kernel(u_ref, w_ref, b_ref, g_ref, beta_ref, a_ref, st_ref, ext_ref):
    j = pl.program_id(1)
    tm = u_ref.shape[0]

    @pl.when(j == 0)
    def _():
        ext_ref[0:_HALO, :] = jnp.zeros((_HALO, C_CONV), F32)

    ext_ref[_HALO:_HALO + tm, :] = u_ref[...]
    off = _HALO - (CONV_WIDTH - 1)
    y = jnp.zeros((tm, C_CONV), F32) + b_ref[...]
    for k in range(CONV_WIDTH):
        y = y + ext_ref[off + k:off + k + tm, :] * w_ref[k:k + 1, :]
    yn = _ln(y) * g_ref[...] + beta_ref[...]
    a_ref[...] = (yn * _sigmoid(yn)).astype(a_ref.dtype)
    st_ref[...] = ext_ref[tm + off:tm + _HALO, :]
    ext_ref[0:_HALO, :] = ext_ref[tm:tm + _HALO, :]


def _conv_prompt(u, w, n_seq, seq_len, tm):
    nt = seq_len // tm
    return pl.pallas_call(
        _conv_prompt_kernel,
        grid=(n_seq, nt),
        in_specs=[pl.BlockSpec((tm, C_CONV), lambda b, j: (b * nt + j, 0)),
                  _const_spec(w["conv_w"].shape), _const_spec((1, C_CONV)), _const_spec((1, C_CONV)),
                  _const_spec((1, C_CONV))],
        out_specs=[pl.BlockSpec((tm, C_CONV), lambda b, j: (b * nt + j, 0)),
                   pl.BlockSpec((None, CONV_WIDTH - 1, C_CONV), lambda b, j: (b, 0, 0))],
        out_shape=[jax.ShapeDtypeStruct((n_seq * seq_len, C_CONV), BF16),
                   jax.ShapeDtypeStruct((n_seq, CONV_WIDTH - 1, C_CONV), F32)],
        scratch_shapes=[pltpu.VMEM((_HALO + tm, C_CONV), F32)],
        compiler_params=_params(("parallel", "arbitrary")),
        name="conv_prompt",
    )(u, w["conv_w"], w["conv_b"], w["conv_ln_g"], w["conv_ln_b"])


def _conv_sample_kernel(u_ref, st_ref, ws_ref, wu_ref, b_ref, g_ref, beta_ref, a_ref, nst_ref):
    n_new = u_ref.shape[0]
    n_st = st_ref.shape[0]
    for t in range(n_new):
        y = jnp.zeros(u_ref.shape[1:], F32) + b_ref[...]
        for r in range(t, n_st):
            y = y + st_ref[r] * ws_ref[t, r:r + 1, :]
        for r in range(t + 1):
            y = y + u_ref[r] * wu_ref[t, r:r + 1, :]
        yn = _ln(y) * g_ref[...] + beta_ref[...]
        a_ref[t] = yn * _sigmoid(yn)
    for r in range(n_st - n_new):
        nst_ref[r] = st_ref[r + n_new]
    for t in range(n_new):
        nst_ref[n_st - n_new + t] = u_ref[t]


def _conv_sample(u_t, st_t, w, nb):
    n_new, n_b, _ = u_t.shape
    n_st = st_t.shape[0]
    return pl.pallas_call(
        _conv_sample_kernel,
        grid=(n_b // nb,),
        in_specs=[pl.BlockSpec((n_new, nb, C_CONV), lambda i: (0, i, 0)),
                  pl.BlockSpec((n_st, nb, C_CONV), lambda i: (0, i, 0)),
                  _const_spec(w["conv_ws"].shape), _const_spec(w["conv_wu"].shape),
                  _const_spec((1, C_CONV)), _const_spec((1, C_CONV)), _const_spec((1, C_CONV))],
        out_specs=[pl.BlockSpec((n_new, nb, C_CONV), lambda i: (0, i, 0)),
                   pl.BlockSpec((n_st, nb, C_CONV), lambda i: (0, i, 0))],
        out_shape=[jax.ShapeDtypeStruct((n_new, n_b, C_CONV), F32),
                   jax.ShapeDtypeStruct((n_st, n_b, C_CONV), F32)],
        compiler_params=_params(("parallel",)),
        name="conv_sample",
    )(u_t, st_t, w["conv_ws"], w["conv_wu"], w["conv_b"], w["conv_ln_g"], w["conv_ln_b"])


def _attn_prompt_kernel(*refs, width, blk, has_bias):
    if has_bias:
        q_ref, k_ref, v_ref, cq_ref, ck_ref, o_ref = refs
    else:
        q_ref, k_ref, v_ref, o_ref = refs
    i = pl.program_id(2)
    lane = lax.broadcasted_iota(jnp.int32, (1, LANES), 1)
    row = lax.broadcasted_iota(jnp.int32, (blk, blk), 0)
    col = lax.broadcasted_iota(jnp.int32, (blk, blk), 1)
    outs = []
    for e in range(2):
        if width == LANES:
            keep = (lane < DH) if e == 0 else (lane >= DH)
            q = jnp.where(keep, q_ref[...], jnp.zeros((), q_ref.dtype))
            lanes = slice(0, LANES)
        else:
            lanes = slice(e * LANES, (e + 1) * LANES)
            q = q_ref[:, lanes]

        def scores(j, q=q, lanes=lanes, e=e):
            start = pl.multiple_of(j * blk, blk)
            s = lax.dot_general(q, k_ref[pl.ds(start, blk), lanes], _NT, preferred_element_type=F32)
            if has_bias:
                s = s + cq_ref[:, e:e + 1] - ck_ref[j, e:e + 1, :]
            return s, start

        def update(carry, s, start):
            m, l, acc = carry
            m_new = jnp.maximum(m, jnp.max(s, axis=1, keepdims=True))
            alpha = jnp.exp(m - m_new)
            p = jnp.exp(s - m_new)
            l = alpha * l + jnp.sum(p, axis=1, keepdims=True)
            acc = alpha * acc + _mm(p.astype(BF16), v_ref[pl.ds(start, blk), :])
            return m_new, l, acc

        def body(j, carry):
            s, start = scores(j)
            return update(carry, s, start)

        init = (jnp.full((blk, 1), NEG_INF, F32), jnp.zeros((blk, 1), F32), jnp.zeros((blk, LANES), F32))
        carry = lax.fori_loop(0, i, body, init)
        s, start = scores(i)
        s = jnp.where(col <= row, s, NEG_INF)
        _, l, acc = update(carry, s, start)
        outs.append(acc / l)
    o_ref[...] = jnp.where(lane < DH, outs[0], outs[1]).astype(o_ref.dtype)


def _attn_prompt(q, k, v, bias, n_seq, seq_len, width, blk):
    n_pairs = N_HEADS // 2
    nq = seq_len // blk
    in_specs = [pl.BlockSpec((blk, width), lambda b, p, i: (b * nq + i, p)),
                pl.BlockSpec((seq_len, width), lambda b, p, i: (b, p)),
                pl.BlockSpec((seq_len, LANES), lambda b, p, i: (b, p))]
    args = [q, k, v]
    if bias is not None:
        in_specs += [pl.BlockSpec((None, None, blk, 2), lambda b, p, i: (b, p, i, 0)),
                     pl.BlockSpec((None, None, nq, 2, blk), lambda b, p, i: (b, p, 0, 0, 0))]
        args += list(bias)
    kern = functools.partial(_attn_prompt_kernel, width=width, blk=blk, has_bias=bias is not None)
    return pl.pallas_call(
        kern, grid=(n_seq, n_pairs, nq), in_specs=in_specs,
        out_specs=pl.BlockSpec((blk, LANES), lambda b, p, i: (b * nq + i, p)),
        out_shape=jax.ShapeDtypeStruct((n_seq * seq_len, n_pairs * LANES), BF16),
        compiler_params=_params(("parallel", "parallel", "arbitrary")),
        name="attn_prompt_fox" if bias is not None else "attn_prompt_mla",
    )(*args)


def _softmax_step(m, l, acc, s, pv):
    m_new = jnp.maximum(m, jnp.max(s, axis=1, keepdims=True))
    alpha = jnp.exp(m - m_new)
    p = jnp.exp(s - m_new)
    return m_new, alpha * l + jnp.sum(p, axis=1, keepdims=True), alpha * acc + pv(p)


def _fox_dec_kernel(pt_ref, q_ref, kn_ref, vn_ref, cc_ref, *rest, n_pg, n_chunks, n_new):
    k_pages = rest[:n_pg]
    v_pages = rest[n_pg:2 * n_pg]
    f_pages = rest[2 * n_pg:3 * n_pg]
    o_ref = rest[3 * n_pg]
    m_ref, l_ref, acc_ref, tot_ref = rest[3 * n_pg + 1:]
    del pt_ref
    c = pl.program_id(1)
    hd = N_HEADS * DH
    n_rows = n_new * N_HEADS
    lane = lax.broadcasted_iota(jnp.int32, (N_HEADS, hd), 1)
    sub = lax.broadcasted_iota(jnp.int32, (N_HEADS, hd), 0)
    hmask = (lane // DH) == sub
    q_new = q_ref[...]
    q_bd = jnp.concatenate(
        [jnp.where(hmask, jnp.broadcast_to(q_new[t:t + 1, :], (N_HEADS, hd)), 0.0) for t in range(n_new)],
        axis=0)
    ccol = cc_ref[...]
    trow = lax.broadcasted_iota(jnp.int32, (n_rows, 1), 0) // N_HEADS

    @pl.when(c == 0)
    def _():
        tot_ref[...] = jnp.zeros_like(tot_ref)
        m = l = acc = None
        for s_idx in range(n_new):
            k_row = kn_ref[s_idx:s_idx + 1, :]
            v_row = vn_ref[s_idx:s_idx + 1, :]
            c_key = jnp.concatenate([ccol[s_idx * N_HEADS:(s_idx + 1) * N_HEADS, :]] * n_new, axis=0)
            sc = jnp.sum(q_bd * k_row, axis=1, keepdims=True) + ccol - c_key
            if s_idx == 0:
                m, l, acc = sc, jnp.ones_like(sc), jnp.broadcast_to(v_row, (n_rows, hd))
            else:
                sc = jnp.where(trow >= s_idx, sc, NEG_INF)
                m, l, acc = _softmax_step(m, l, acc, sc, lambda p, v_row=v_row: p * v_row)
        m_ref[...] = m
        l_ref[...] = l
        acc_ref[...] = acc

    m = m_ref[...]
    l = l_ref[...]
    acc = acc_ref[...]
    tot = tot_ref[...]
    qb = q_bd.astype(BF16)
    lane_pg = lax.broadcasted_iota(jnp.int32, (N_HEADS, LANES), 1)
    for i in reversed(range(n_pg)):
        lf_t = f_pages[i][...]
        inc = lf_t
        d = 1
        while d < LANES:
            inc = inc + jnp.where(lane_pg < LANES - d, pltpu.roll(inc, LANES - d, axis=1), 0.0)
            d *= 2
        suffix = inc - lf_t + tot
        tot = tot + inc[:, 0:1]
        s = _mm(qb, k_pages[i][...].astype(BF16))
        s = s + ccol + jnp.concatenate([suffix] * n_new, axis=0)
        vb = v_pages[i][...].astype(BF16)
        m, l, acc = _softmax_step(
            m, l, acc, s, lambda p, vb=vb: lax.dot_general(p.astype(BF16), vb, _NT, preferred_element_type=F32))
    m_ref[...] = m
    l_ref[...] = l
    acc_ref[...] = acc
    tot_ref[...] = tot

    @pl.when(c == n_chunks - 1)
    def _():
        o = acc / l
        for t in range(n_new):
            o_ref[t:t + 1, :] = jnp.sum(jnp.where(hmask, o[t * N_HEADS:(t + 1) * N_HEADS, :], 0.0),
                                        axis=0, keepdims=True)


def _page_specs(layer, block, n_pg, n_chunks):
    def make(i):
        def index_map(b, c, pt):
            return (layer, pt[b, (n_chunks - 1 - c) * n_pg + i]) + (0,) * len(block)
        return pl.BlockSpec((None, None) + block, index_map)
    return [make(i) for i in range(n_pg)]


def _fox_dec(page_table, q, k_new, v_new, ccol, cache_k, cache_v, cache_f, layer, n_pg):
    n_b, n_new, hd = q.shape
    n_pages = page_table.shape[1]
    n_chunks = n_pages // n_pg
    page = cache_k.shape[3]
    n_rows = n_new * N_HEADS
    small = lambda shape: pl.BlockSpec((None,) + shape, lambda b, c, pt: (b,) + (0,) * len(shape))
    in_specs = ([small((n_new, hd)), small((n_new, hd)), small((n_new, hd)), small((n_rows, 1))]
                + _page_specs(layer, (hd, page), n_pg, n_chunks) * 2
                + _page_specs(layer, (N_HEADS, page), n_pg, n_chunks))
    kern = functools.partial(_fox_dec_kernel, n_pg=n_pg, n_chunks=n_chunks, n_new=n_new)
    return pl.pallas_call(
        kern,
        grid_spec=pltpu.PrefetchScalarGridSpec(
            num_scalar_prefetch=1, grid=(n_b, n_chunks), in_specs=in_specs,
            out_specs=pl.BlockSpec((None, n_new, hd), lambda b, c, pt: (b, 0, 0)),
            scratch_shapes=[pltpu.VMEM((n_rows, 1), F32), pltpu.VMEM((n_rows, 1), F32),
                            pltpu.VMEM((n_rows, hd), F32), pltpu.VMEM((N_HEADS, 1), F32)]),
        out_shape=jax.ShapeDtypeStruct((n_b, n_new, hd), F32),
        compiler_params=_params(("parallel", "arbitrary")),
        name="fox_decode",
    )(page_table, q, k_new, v_new, ccol, *([cache_k] * n_pg), *([cache_v] * n_pg), *([cache_f] * n_pg))


def _mla_dec_kernel(pt_ref, qa_ref, qr_ref, ln_ref, rn_ref, *rest, n_pg, n_chunks, n_new):
    l_pages = rest[:n_pg]
    r_pages = rest[n_pg:2 * n_pg]
    o_ref = rest[2 * n_pg]
    m_ref, l_ref, acc_ref = rest[2 * n_pg + 1:]
    del pt_ref
    c = pl.program_id(1)
    n_rows = n_new * N_HEADS
    qa = qa_ref[...]
    qr = qr_ref[...]
    trow = lax.broadcasted_iota(jnp.int32, (n_rows, 1), 0) // N_HEADS

    @pl.when(c == 0)
    def _():
        m = l = acc = None
        for s_idx in range(n_new):
            lat_row = ln_ref[s_idx:s_idx + 1, :]
            kr_row = rn_ref[s_idx:s_idx + 1, :]
            sc = (jnp.sum(qa * lat_row, axis=1, keepdims=True)
                  + jnp.sum(qr * kr_row, axis=1, keepdims=True))
            if s_idx == 0:
                m, l, acc = sc, jnp.ones_like(sc), jnp.broadcast_to(lat_row, (n_rows, KV_LORA))
            else:
                sc = jnp.where(trow >= s_idx, sc, NEG_INF)
                m, l, acc = _softmax_step(m, l, acc, sc, lambda p, lat_row=lat_row: p * lat_row)
        m_ref[...] = m
        l_ref[...] = l
        acc_ref[...] = acc

    m = m_ref[...]
    l = l_ref[...]
    acc = acc_ref[...]
    qab = qa.astype(BF16)
    qrb = qr.astype(BF16)
    for i in range(n_pg):
        latb = l_pages[i][...].astype(BF16)
        s = (lax.dot_general(qab, latb, _NT, preferred_element_type=F32)
             + _mm(qrb, r_pages[i][...].astype(BF16)))
        m, l, acc = _softmax_step(m, l, acc, s, lambda p, latb=latb: _mm(p.astype(BF16), latb))
    m_ref[...] = m
    l_ref[...] = l
    acc_ref[...] = acc

    @pl.when(c == n_chunks - 1)
    def _():
        o_ref[...] = acc / l


def _mla_dec(page_table, qabs, qrope, lat_new, kr_new, cache_lat, cache_kr, layer, n_pg):
    n_b, n_rows, _ = qabs.shape
    n_new = lat_new.shape[1]
    n_pages = page_table.shape[1]
    n_chunks = n_pages // n_pg
    page = cache_lat.shape[2]
    small = lambda shape: pl.BlockSpec((None,) + shape, lambda b, c, pt: (b,) + (0,) * len(shape))
    in_specs = ([small((n_rows, KV_LORA)), small((n_rows, ROPE)), small((n_new, KV_LORA)), small((n_new, ROPE))]
                + _page_specs(layer, (page, KV_LORA), n_pg, n_chunks)
                + _page_specs(layer, (ROPE, page), n_pg, n_chunks))
    kern = functools.partial(_mla_dec_kernel, n_pg=n_pg, n_chunks=n_chunks, n_new=n_new)
    return pl.pallas_call(
        kern,
        grid_spec=pltpu.PrefetchScalarGridSpec(
            num_scalar_prefetch=1, grid=(n_b, n_chunks), in_specs=in_specs,
            out_specs=pl.BlockSpec((None, n_rows, KV_LORA), lambda b, c, pt: (b, 0, 0)),
            scratch_shapes=[pltpu.VMEM((n_rows, 1), F32), pltpu.VMEM((n_rows, 1), F32),
                            pltpu.VMEM((n_rows, KV_LORA), F32)]),
        out_shape=jax.ShapeDtypeStruct((n_b, n_rows, KV_LORA), F32),
        compiler_params=_params(("parallel", "arbitrary")),
        name="mla_decode",
    )(page_table, qabs, qrope, lat_new, kr_new, *([cache_lat] * n_pg), *([cache_kr] * n_pg))


def _merge_kernel(*refs, latent_out, alpha):
    (x_ref, shift_ref, scale_ref, gate_ref, a_ref, of_ref, om_ref, wg_ref, wc_ref, wf_ref) = refs[:10]
    refs = refs[10:]
    if latent_out:
        bduv_ref = refs[0]
        refs = refs[1:]
    wm_ref, wo_ref, g_ref, b_ref, o_ref = refs
    x = x_ref[...]
    h = (_ln(x) * (1.0 + scale_ref[...]) + shift_ref[...]).astype(BF16)
    y_conv = _mm(a_ref[...].astype(BF16), wc_ref[...])
    y_fox = _mm(of_ref[...].astype(BF16), wf_ref[...])
    om = om_ref[...].astype(BF16)
    if latent_out:
        om = _mm(om, bduv_ref[...]).astype(BF16)
    y_mla = _mm(om, wm_ref[...])
    merged = (_sigmoid(_mm(h, wg_ref[:, 0:D_MODEL])) * y_conv
              + _sigmoid(_mm(h, wg_ref[:, D_MODEL:2 * D_MODEL])) * y_fox
              + _sigmoid(_mm(h, wg_ref[:, 2 * D_MODEL:3 * D_MODEL])) * y_mla)
    mix = _mm(merged.astype(BF16), wo_ref[...])
    o_ref[...] = _ln(alpha * x + gate_ref[...] * mix) * g_ref[...] + b_ref[...]


def _merge(x, mod_arr, mod_specs, a, o_fox, o_mla, w, *, latent_out, tm, alpha):
    rows = x.shape[0]
    row_spec = lambda width: pl.BlockSpec((tm, width), lambda i: (i, 0))
    weights = [w["w_gate"], w["conv_w_out"], w["fox_w_o"]] + ([w["bd_uv"]] if latent_out else []) + [
        w["mla_w_o"], w["w_out"], w["ln1_g"], w["ln1_b"]]
    in_specs = ([row_spec(D_MODEL)] + list(mod_specs) + [row_spec(a.shape[1]), row_spec(o_fox.shape[1]),
                                                         row_spec(o_mla.shape[1])]
                + [_const_spec(t.shape) for t in weights])
    kern = functools.partial(_merge_kernel, latent_out=latent_out, alpha=alpha)
    return pl.pallas_call(
        kern, grid=(rows // tm,), in_specs=in_specs, out_specs=row_spec(D_MODEL),
        out_shape=jax.ShapeDtypeStruct((rows, D_MODEL), F32),
        compiler_params=_params(("parallel",)),
        name="merge",
    )(x, mod_arr, mod_arr, mod_arr, a, o_fox, o_mla, *weights)


def _ffn_kernel(x_ref, shift_ref, scale_ref, gate_ref, wu_ref, wd_ref, g_ref, b_ref, o_ref, *, alpha, chunk):
    x = x_ref[...]
    h = (_ln(x) * (1.0 + scale_ref[...]) + shift_ref[...]).astype(BF16)
    ff = jnp.zeros(x.shape, F32)
    for s in range(0, D_FF, chunk):
        t = jnp.maximum(_mm(h, wu_ref[:, s:s + chunk]), 0.0)
        ff = ff + _mm((t * t).astype(BF16), wd_ref[s:s + chunk, :])
    o_ref[...] = _ln(alpha * x + gate_ref[...] * ff) * g_ref[...] + b_ref[...]


def _ffn(x, mod_arr, mod_specs, w, *, tm, alpha):
    rows = x.shape[0]
    row_spec = pl.BlockSpec((tm, D_MODEL), lambda i: (i, 0))
    weights = [w["w_up"], w["w_down"], w["ln2_g"], w["ln2_b"]]
    kern = functools.partial(_ffn_kernel, alpha=alpha, chunk=1024)
    return pl.pallas_call(
        kern, grid=(rows // tm,), in_specs=[row_spec] + list(mod_specs) + [_const_spec(t.shape) for t in weights],
        out_specs=row_spec, out_shape=jax.ShapeDtypeStruct((rows, D_MODEL), F32),
        compiler_params=_params(("parallel",)),
        name="ffn",
    )(x, mod_arr, mod_arr, mod_arr, *weights)


def _swap_halves(w):
    half = w.shape[-1] // 2
    return jnp.concatenate([w[..., half:], w[..., :half]], axis=-1)


def _block_diag(blocks):
    n, r, c = blocks.shape
    eye = jnp.eye(n, dtype=blocks.dtype)
    return (eye[:, None, :, None] * blocks[:, :, None, :]).reshape(n * r, n * c)


def _layer_weights(p, n_new):
    w_in = p["w_in"]
    hd = N_HEADS * DH
    o_f = 2 * C_CONV + 3 * hd
    o_ql = o_f + N_HEADS
    o_kv = o_ql + Q_LORA
    o_kr = o_kv + KV_LORA
    o_g = o_kr + ROPE
    w_f = w_in[:, o_f:o_ql]
    w_kr = w_in[:, o_kr:o_g]
    w_krr = _swap_halves(w_kr)
    zeros = lambda n: jnp.zeros((D_MODEL, n), F32)
    w = {}
    w["w_main"] = w_in[:, :o_f].astype(BF16)
    w["w_ql"] = w_in[:, o_ql:o_kv].astype(BF16)
    w["w_kv"] = w_in[:, o_kv:o_kr].astype(BF16)
    w["w_s"] = jnp.concatenate([w_kr, w_krr, w_f, zeros(LANES - 2 * ROPE - N_HEADS)], axis=1).astype(BF16)
    w["b_f"] = jnp.zeros((1, LANES), F32).at[0, 2 * ROPE:2 * ROPE + N_HEADS].set(p["b_f"])
    w["g_q"] = p["mla_q_norm_g"].reshape(1, Q_LORA)
    w["g_kv"] = p["mla_kv_norm_g"].reshape(1, KV_LORA)
    w["w_gate"] = w_in[:, o_g:].astype(BF16)
    w["w_kr"] = jnp.concatenate([zeros(DH), w_kr, zeros(HEAD_PAD - DH - ROPE)], axis=1).astype(BF16)
    w["w_krr"] = jnp.concatenate([zeros(DH), w_krr, zeros(HEAD_PAD - DH - ROPE)], axis=1).astype(BF16)
    w_qb = p["mla_w_qb"].reshape(Q_LORA, N_HEADS, DH + ROPE)
    qz = lambda n: jnp.zeros((Q_LORA, N_HEADS, n), F32)
    w["w_qa"] = jnp.concatenate([w_qb, qz(HEAD_PAD - DH - ROPE)], axis=2).reshape(Q_LORA, -1).astype(BF16)
    w["w_qb"] = jnp.concatenate([qz(DH), _swap_halves(w_qb[..., DH:]), qz(HEAD_PAD - DH - ROPE)],
                                axis=2).reshape(Q_LORA, -1).astype(BF16)
    w_uk = p["mla_w_uk"]
    w["w_ka"] = jnp.concatenate([w_uk, jnp.zeros((KV_LORA, N_HEADS, HEAD_PAD - DH), F32)],
                                axis=2).reshape(KV_LORA, -1).astype(BF16)
    w["w_uv"] = p["mla_w_uv"].reshape(KV_LORA, hd).astype(BF16)
    w["w_qn"] = w_qb[..., :DH].reshape(Q_LORA, hd).astype(BF16)
    w["w_qr"] = w_qb[..., DH:].reshape(Q_LORA, N_HEADS * ROPE).astype(BF16)
    w["w_qrr"] = _swap_halves(w_qb[..., DH:]).reshape(Q_LORA, N_HEADS * ROPE).astype(BF16)
    w["bd_uk"] = _block_diag(jnp.transpose(w_uk, (1, 2, 0))).astype(BF16)
    w["bd_uv"] = _block_diag(jnp.transpose(p["mla_w_uv"], (1, 0, 2))).astype(BF16)
    cw = p["conv_w"]
    w["conv_w"] = jnp.concatenate([cw, jnp.zeros((_HALO - CONV_WIDTH, C_CONV), F32)], axis=0)
    n_st = CONV_WIDTH - 1
    idx = jnp.arange(n_st + n_new)[None, :] - jnp.arange(n_new)[:, None]
    taps = jnp.where(((idx >= 0) & (idx < CONV_WIDTH))[..., None], cw[jnp.clip(idx, 0, CONV_WIDTH - 1)], 0.0)
    w["conv_ws"] = taps[:, :n_st]
    w["conv_wu"] = taps[:, n_st:]
    for name in ("conv_b", "conv_ln_g", "conv_ln_b"):
        w[name] = p[name].reshape(1, C_CONV)
    for name in ("conv_w_out", "fox_w_o", "mla_w_o", "w_out", "w_up", "w_down"):
        w[name] = p[name].astype(BF16)
    for name in ("ln1_g", "ln1_b", "ln2_g", "ln2_b"):
        w[name] = p[name].reshape(1, D_MODEL)
    return w


def _rope_tables(pos, reps):
    half = ROPE // 2
    inv_freq = ROPE_BASE ** (-jnp.arange(half, dtype=F32) / half)
    ang = pos.astype(F32)[:, None] * inv_freq
    cos, sin = jnp.cos(ang), jnp.sin(ang)
    cos32 = jnp.concatenate([cos, cos], axis=1)
    sin32 = jnp.concatenate([-sin, sin], axis=1)
    n = pos.shape[0]
    tabs = {"cos32": cos32, "sin32": sin32,
            "cos128": jnp.concatenate([jnp.ones((n, DH), F32), cos32, jnp.zeros((n, HEAD_PAD - DH - ROPE), F32)], 1),
            "sin128": jnp.concatenate([jnp.zeros((n, DH), F32), sin32, jnp.zeros((n, HEAD_PAD - DH - ROPE), F32)], 1),
            "cos256": jnp.tile(cos32, (1, N_HEADS)), "sin256": jnp.tile(sin32, (1, N_HEADS))}
    if reps > 1:
        tabs = {k: jnp.tile(v, (reps, 1)) for k, v in tabs.items()}
    return tabs


def _mod_specs_seq(cols, tiles_per_seq):
    return [pl.BlockSpec((None, 1, D_MODEL), functools.partial(
        lambda i, c: (i // tiles_per_seq, 0, c), c=c)) for c in cols]


def _mod_specs_rows(cols, tm):
    return [pl.BlockSpec((tm, D_MODEL), functools.partial(lambda i, c: (i, c), c=c)) for c in cols]


def kernel(x_prompt, x_sample, cache_fox_k, cache_fox_v, cache_fox_logf, cache_mla_latent, cache_mla_krope,
           state_conv, page_table, c_prompt, c_sample, w_ada, b_ada, w_in, b_f, conv_w, conv_b, conv_ln_g,
           conv_ln_b, conv_w_out, fox_w_o, mla_q_norm_g, mla_w_qb, mla_kv_norm_g, mla_w_uk, mla_w_uv, mla_w_o,
           w_out, ln1_g, ln1_b, w_up, w_down, ln2_g, ln2_b):
    n_layers = w_in.shape[0]
    n_b, seq_len, _ = x_prompt.shape
    n_s, n_new, _ = x_sample.shape
    page = cache_fox_k.shape[2]
    n_pages = page_table.shape[1]
    n_past = n_pages * page
    alpha = float((2 * n_layers) ** 0.25)
    hd = N_HEADS * DH
    n_pairs = N_HEADS // 2

    tm_p = min(512, seq_len)
    rows_s = n_s * n_new
    tm_s = min(256, rows_s)
    blk = min(256, seq_len)
    n_pg = min(8, n_pages)
    nb_conv = min(8, n_s)

    n_pool = cache_fox_k.shape[1]
    ck = jnp.transpose(cache_fox_k, (0, 1, 3, 4, 2)).reshape(n_layers, n_pool, hd, page)
    cv = jnp.transpose(cache_fox_v, (0, 1, 3, 4, 2)).reshape(n_layers, n_pool, hd, page)
    cf = jnp.transpose(cache_fox_logf, (0, 1, 3, 2))
    ckr = jnp.transpose(cache_mla_krope, (0, 1, 3, 2))

    m_all = n_b + n_s
    m_pad = -(-m_all // 8) * 8
    c_all = jnp.concatenate([c_prompt, c_sample, jnp.zeros((m_pad - m_all, D_MODEL), F32)], axis=0)
    mod_all = _ada(c_all, w_ada, b_ada)

    tabs_p = _rope_tables(jnp.arange(seq_len, dtype=jnp.int32), 1)
    tabs_s = _rope_tables(n_past + jnp.arange(n_new, dtype=jnp.int32), n_s)

    params = dict(w_in=w_in, b_f=b_f, conv_w=conv_w, conv_b=conv_b, conv_ln_g=conv_ln_g, conv_ln_b=conv_ln_b,
                  conv_w_out=conv_w_out, fox_w_o=fox_w_o, mla_q_norm_g=mla_q_norm_g, mla_w_qb=mla_w_qb,
                  mla_kv_norm_g=mla_kv_norm_g, mla_w_uk=mla_w_uk, mla_w_uv=mla_w_uv, mla_w_o=mla_w_o,
                  w_out=w_out, ln1_g=ln1_g, ln1_b=ln1_b, w_up=w_up, w_down=w_down, ln2_g=ln2_g, ln2_b=ln2_b)

    xp = x_prompt.reshape(n_b * seq_len, D_MODEL)
    xs = x_sample.reshape(rows_s, D_MODEL)
    rows_p, rows_smp = [], []
    tps = seq_len // tm_p
    for l in range(n_layers):
        w = _layer_weights({k: v[l] for k, v in params.items()}, n_new)
        mod_p = mod_all[l, :n_b].reshape(n_b, 1, 6 * D_MODEL)
        mod_s = jnp.repeat(mod_all[l, n_b:n_b + n_s], n_new, axis=0)

        (u, kf, vf, logf, cs, lat, kr, qf, kb, vb, mq, mk, mv) = _inproj(
            xp, mod_p, _mod_specs_seq((0, 1), tps), w, tabs_p, prompt=True, seg=seq_len, tm=tm_p)
        a_p, st_p = _conv_prompt(u, w, n_b, seq_len, tm_p)
        c4 = cs.reshape(n_b, seq_len, n_pairs, 2)
        cq = jnp.transpose(c4, (0, 2, 1, 3))
        ckk = jnp.transpose(c4.reshape(n_b, seq_len // blk, blk, n_pairs, 2), (0, 3, 1, 4, 2))
        o_fox = _attn_prompt(qf, kb, vb, (cq, ckk), n_b, seq_len, LANES, blk)
        o_mla = _attn_prompt(mq, mk, mv, None, n_b, seq_len, 2 * HEAD_PAD, blk)
        x1 = _merge(xp, mod_p, _mod_specs_seq((0, 1, 2), tps), a_p, o_fox, o_mla, w,
                    latent_out=False, tm=tm_p, alpha=alpha)
        xp = _ffn(x1, mod_p, _mod_specs_seq((3, 4, 5), tps), w, tm=tm_p, alpha=alpha)
        rows_p.append((kf.reshape(n_b, seq_len, N_HEADS, DH), vf.reshape(n_b, seq_len, N_HEADS, DH),
                       logf.reshape(n_b, seq_len, N_HEADS), lat.reshape(n_b, seq_len, KV_LORA),
                       kr.reshape(n_b, seq_len, ROPE), st_p))

        (u, kf, vf, logf, cs, lat, kr, qf, qabs, qrope) = _inproj(
            xs, mod_s, _mod_specs_rows((0, 1), tm_s), w, tabs_s, prompt=False, seg=n_new, tm=tm_s)
        u_t = jnp.transpose(u.reshape(n_s, n_new, C_CONV), (1, 0, 2))
        st_t = jnp.transpose(state_conv[l], (1, 0, 2))
        a_t, nst_t = _conv_sample(u_t, st_t, w, nb_conv)
        a_s = jnp.transpose(a_t, (1, 0, 2)).reshape(rows_s, C_CONV)
        st_s = jnp.transpose(nst_t, (1, 0, 2))
        o_fox = _fox_dec(page_table, qf.reshape(n_s, n_new, hd), kf.reshape(n_s, n_new, hd),
                         vf.reshape(n_s, n_new, hd), cs.reshape(n_s, n_new * N_HEADS, 1),
                         ck, cv, cf, l, n_pg)
        o_lat = _mla_dec(page_table, qabs.reshape(n_s, n_new * N_HEADS, KV_LORA),
                         qrope.reshape(n_s, n_new * N_HEADS, ROPE), lat.reshape(n_s, n_new, KV_LORA),
                         kr.reshape(n_s, n_new, ROPE), cache_mla_latent, ckr, l, n_pg)
        x1 = _merge(xs, mod_s, _mod_specs_rows((0, 1, 2), tm_s), a_s, o_fox.reshape(rows_s, hd),
                    o_lat.reshape(rows_s, N_HEADS * KV_LORA), w, latent_out=True, tm=tm_s, alpha=alpha)
        xs = _ffn(x1, mod_s, _mod_specs_rows((3, 4, 5), tm_s), w, tm=tm_s, alpha=alpha)
        rows_smp.append((kf.reshape(n_s, n_new, N_HEADS, DH), vf.reshape(n_s, n_new, N_HEADS, DH),
                         logf.reshape(n_s, n_new, N_HEADS), lat.reshape(n_s, n_new, KV_LORA),
                         kr.reshape(n_s, n_new, ROPE), st_s))

    def stk(rows, i):
        return jnp.stack([r[i] for r in rows])

    return ((xp.reshape(n_b, seq_len, D_MODEL), xs.reshape(n_s, n_new, D_MODEL))
            + tuple(stk(rows_p, i) for i in range(6)) + tuple(stk(rows_smp, i) for i in range(6)))
```

```python
import functools
import math

import jax
import jax.numpy as jnp
from jax import lax
from jax.experimental import pallas as pl
from jax.experimental.pallas import tpu as pltpu

F32 = jnp.float32
BF16 = jnp.bfloat16

D_MODEL = 1024
C_CONV = D_MODEL // 2
CONV_WIDTH = 31
N_HEADS = 8
DH = D_MODEL // 16
ROPE = D_MODEL // 32
KV_LORA = D_MODEL // 4
Q_LORA = 3 * KV_LORA
D_FF = 4 * D_MODEL
ROPE_BASE = 10000.0
NEG_INF = -1e30
LANES = 128
HEAD_PAD = 128
VMEM_LIMIT = 56 * 1024 * 1024

_NT = (((1,), (1,)), ((), ()))


def _params(sem, vmem=VMEM_LIMIT):
    return pltpu.CompilerParams(dimension_semantics=sem, vmem_limit_bytes=vmem)


def _const_spec(shape):
    nd = len(shape)
    return pl.BlockSpec(shape, lambda *_: (0,) * nd, pipeline_mode=pl.Buffered(1))


def _ln(x, eps=1e-5):
    mu = jnp.mean(x, axis=-1, keepdims=True)
    xc = x - mu
    var = jnp.mean(xc * xc, axis=-1, keepdims=True)
    return xc * lax.rsqrt(var + eps)


def _rms(x, eps=1e-6):
    return x * lax.rsqrt(jnp.mean(x * x, axis=-1, keepdims=True) + eps)


def _sigmoid(x):
    return 1.0 / (1.0 + jnp.exp(-x))


def _log_sigmoid(x):
    return jnp.minimum(x, 0.0) - jnp.log(1.0 + jnp.exp(-jnp.abs(x)))


def _mm(a, b):
    return jnp.dot(a, b, preferred_element_type=F32)


def _ada_kernel(c_ref, w_ref, b_ref, o_ref):
    c = c_ref[...]
    a = (c * _sigmoid(c)).astype(BF16)
    o_ref[...] = _mm(a, w_ref[...].astype(BF16)) + b_ref[...]


def _ada(c_all, w_ada, b_ada):
    n_layers, _, n_out = w_ada.shape
    m = c_all.shape[0]
    tn = 1536
    return pl.pallas_call(
        _ada_kernel,
        grid=(n_layers, n_out // tn),
        in_specs=[pl.BlockSpec((m, D_MODEL), lambda l, j: (0, 0)),
                  pl.BlockSpec((None, D_MODEL, tn), lambda l, j: (l, 0, j)),
                  pl.BlockSpec((None, 1, tn), lambda l, j: (l, 0, j))],
        out_specs=pl.BlockSpec((None, m, tn), lambda l, j: (l, 0, j)),
        out_shape=jax.ShapeDtypeStruct((n_layers, m, n_out), F32),
        compiler_params=_params(("parallel", "parallel")),
        name="ada_mod",
    )(c_all, w_ada, b_ada.reshape(n_layers, 1, n_out))


def _seg_cumsum(x, seg):
    rows = x.shape[0]
    row = lax.broadcasted_iota(jnp.int32, x.shape, 0)
    rin = row % seg if seg < rows else row
    d = 1
    while d < min(seg, rows):
        x = x + jnp.where(rin >= d, pltpu.roll(x, d, axis=0), 0.0)
        d *= 2
    return x


def _inproj_kernel(*refs, prompt, seg, tiles_per_seq, mla_scale):
    (x_ref, shift_ref, scale_ref, wmain_ref, wql_ref, wkv_ref, ws_ref, bf_ref, gq_ref, gkv_ref,
     cos32_ref, sin32_ref) = refs[:12]
    refs = refs[12:]
    if prompt:
        (wkr_ref, wkrr_ref, cos128_ref, sin128_ref, wqa_ref, wqb_ref, wka_ref, wuv_ref) = refs[:8]
        refs = refs[8:]
        (u_ref, kf_ref, vf_ref, logf_ref, c_ref, lat_ref, kr_ref,
         qf_ref, kb_ref, vb_ref, mq_ref, mk_ref, mv_ref, carry_ref) = refs
    else:
        (wqn_ref, bduk_ref, wqr_ref, wqrr_ref, cos256_ref, sin256_ref) = refs[:6]
        refs = refs[6:]
        (u_ref, kf_ref, vf_ref, logf_ref, c_ref, lat_ref, kr_ref,
         qf_ref, qabs_ref, qrope_ref, carry_ref) = refs

    i = pl.program_id(0)
    x = x_ref[...]
    tm = x.shape[0]
    h = (_ln(x) * (1.0 + scale_ref[...]) + shift_ref[...]).astype(BF16)

    u_ref[...] = _mm(h, wmain_ref[:, 0:C_CONV]) * _sigmoid(_mm(h, wmain_ref[:, C_CONV:2 * C_CONV]))
    o = 2 * C_CONV
    hd = N_HEADS * DH
    qf_ref[...] = (_mm(h, wmain_ref[:, o:o + hd]) * (DH ** -0.5)).astype(qf_ref.dtype)
    kf = _mm(h, wmain_ref[:, o + hd:o + 2 * hd])
    vf = _mm(h, wmain_ref[:, o + 2 * hd:o + 3 * hd])
    kf_ref[...] = kf
    vf_ref[...] = vf
    if prompt:
        kb_ref[...] = kf.astype(BF16)
        vb_ref[...] = vf.astype(BF16)

    zs = _mm(h, ws_ref[...])
    kr_ref[...] = zs[:, 0:ROPE] * cos32_ref[...] + zs[:, ROPE:2 * ROPE] * sin32_ref[...]
    logf = _log_sigmoid(zs + bf_ref[...])
    logf_ref[...] = logf[:, 2 * ROPE:2 * ROPE + N_HEADS]
    cs = _seg_cumsum(logf, seg)
    if seg >= tm:
        @pl.when(i % tiles_per_seq == 0)
        def _():
            carry_ref[...] = jnp.zeros_like(carry_ref)
        cs = cs + carry_ref[...]
        carry_ref[...] = cs[tm - 1:tm, :]
    c_ref[...] = cs[:, 2 * ROPE:2 * ROPE + N_HEADS]

    latn = _rms(_mm(h, wkv_ref[...])) * gkv_ref[...]
    lat_ref[...] = latn
    qn = (_rms(_mm(h, wql_ref[...])) * gq_ref[...]).astype(BF16)
    if prompt:
        latb = latn.astype(BF16)
        cosq = cos128_ref[...]
        sinq = sin128_ref[...]
        krt = _mm(h, wkr_ref[...]) * cosq + _mm(h, wkrr_ref[...]) * sinq
        for hh in range(N_HEADS):
            sl = slice(hh * HEAD_PAD, (hh + 1) * HEAD_PAD)
            za = _mm(qn, wqa_ref[:, sl])
            zb = _mm(qn, wqb_ref[:, sl])
            mq_ref[:, sl] = ((za * cosq + zb * sinq) * mla_scale).astype(BF16)
            mk_ref[:, sl] = (_mm(latb, wka_ref[:, sl]) + krt).astype(BF16)
        mv_ref[...] = _mm(latb, wuv_ref[...]).astype(BF16)
    else:
        qnope = _mm(qn, wqn_ref[...]).astype(BF16)
        qabs_ref[...] = _mm(qnope, bduk_ref[...]) * mla_scale
        qrope_ref[...] = (_mm(qn, wqr_ref[...]) * cos256_ref[...]
                          + _mm(qn, wqrr_ref[...]) * sin256_ref[...]) * mla_scale


def _inproj(x, shift_spec_arr, mod_specs, w, tabs, *, prompt, seg, tm):
    rows = x.shape[0]
    n_tiles = rows // tm
    n_tab = tabs["cos32"].shape[0] // tm
    tiles_per_seq = max(seg // tm, 1)

    def row_spec(width):
        return pl.BlockSpec((tm, width), lambda i: (i, 0))

    def tab_spec(width):
        return pl.BlockSpec((tm, width), lambda i: (i % n_tab, 0))

    common_w = [w["w_main"], w["w_ql"], w["w_kv"], w["w_s"], w["b_f"], w["g_q"], w["g_kv"]]
    in_arrays = [x, shift_spec_arr, shift_spec_arr] + common_w + [tabs["cos32"], tabs["sin32"]]
    in_specs = ([row_spec(D_MODEL), mod_specs[0], mod_specs[1]]
                + [_const_spec(a.shape) for a in common_w] + [tab_spec(ROPE), tab_spec(ROPE)])
    hd = N_HEADS * DH
    out_shapes = [jax.ShapeDtypeStruct((rows, C_CONV), F32),
                  jax.ShapeDtypeStruct((rows, hd), F32),
                  jax.ShapeDtypeStruct((rows, hd), F32),
                  jax.ShapeDtypeStruct((rows, N_HEADS), F32),
                  jax.ShapeDtypeStruct((rows, N_HEADS), F32),
                  jax.ShapeDtypeStruct((rows, KV_LORA), F32),
                  jax.ShapeDtypeStruct((rows, ROPE), F32)]
    out_specs = [row_spec(C_CONV), row_spec(hd), row_spec(hd), row_spec(N_HEADS), row_spec(N_HEADS),
                 row_spec(KV_LORA), row_spec(ROPE)]
    if prompt:
        extra_w = [w["w_kr"], w["w_krr"]]
        in_arrays += extra_w + [tabs["cos128"], tabs["sin128"]]
        in_specs += [_const_spec(a.shape) for a in extra_w] + [tab_spec(LANES), tab_spec(LANES)]
        extra_w2 = [w["w_qa"], w["w_qb"], w["w_ka"], w["w_uv"]]
        in_arrays += extra_w2
        in_specs += [_const_spec(a.shape) for a in extra_w2]
        wide = N_HEADS * HEAD_PAD
        out_shapes += [jax.ShapeDtypeStruct((rows, hd), BF16)] * 3 + [
            jax.ShapeDtypeStruct((rows, wide), BF16), jax.ShapeDtypeStruct((rows, wide), BF16),
            jax.ShapeDtypeStruct((rows, hd), BF16)]
        out_specs += [row_spec(hd)] * 3 + [row_spec(wide), row_spec(wide), row_spec(hd)]
    else:
        extra_w = [w["w_qn"], w["bd_uk"], w["w_qr"], w["w_qrr"]]
        in_arrays += extra_w + [tabs["cos256"], tabs["sin256"]]
        in_specs += [_const_spec(a.shape) for a in extra_w] + [tab_spec(N_HEADS * ROPE)] * 2
        out_shapes += [jax.ShapeDtypeStruct((rows, hd), F32),
                       jax.ShapeDtypeStruct((rows, N_HEADS * KV_LORA), F32),
                       jax.ShapeDtypeStruct((rows, N_HEADS * ROPE), F32)]
        out_specs += [row_spec(hd), row_spec(N_HEADS * KV_LORA), row_spec(N_HEADS * ROPE)]

    kern = functools.partial(_inproj_kernel, prompt=prompt, seg=seg, tiles_per_seq=tiles_per_seq,
                             mla_scale=float((DH + ROPE) ** -0.5))
    return pl.pallas_call(
        kern, grid=(n_tiles,), in_specs=in_specs, out_specs=out_specs, out_shape=out_shapes,
        scratch_shapes=[pltpu.VMEM((1, LANES), F32)],
        compiler_params=_params(("arbitrary",)),
        name="inproj_prompt" if prompt else "inproj_sample",
    )(*in_arrays)


_HALO = 32


def _conv_prompt_kernel(u_ref, w_ref, b_ref, g_ref, beta_ref, a_ref, st_ref, ext_ref):
    j = pl.program_id(1)
    tm = u_ref.shape[0]

    @pl.when(j == 0)
    def _():
        ext_ref[0:_HALO, :] = jnp.zeros((_HALO, C_CONV), F32)

    ext_ref[_HALO:_HALO + tm, :] = u_ref[...]
    off = _HALO - (CONV_WIDTH - 1)
    y = jnp.zeros((tm, C_CONV), F32) + b_ref[...]
    for k in range(CONV_WIDTH):
        y = y + ext_ref[off + k:off + k + tm, :] * w_ref[k:k + 1, :]
    yn = _ln(y) * g_ref[...] + beta_ref[...]
    a_ref[...] = (yn * _sigmoid(yn)).astype(a_ref.dtype)
    st_ref[...] = ext_ref[tm + off:tm + _HALO, :]
    ext_ref[0:_HALO, :] = ext_ref[tm:tm + _HALO, :]


def _conv_prompt(u, w, n_seq, seq_len, tm):
    nt = seq_len // tm
    return pl.pallas_call(
        _conv_prompt_kernel,
        grid=(n_seq, nt),
        in_specs=[pl.BlockSpec((tm, C_CONV), lambda b, j: (b * nt + j, 0)),
                  _const_spec(w["conv_w"].shape), _const_spec((1, C_CONV)), _const_spec((1, C_CONV)),
                  _const_spec((1, C_CONV))],
        out_specs=[pl.BlockSpec((tm, C_CONV), lambda b, j: (b * nt + j, 0)),
                   pl.BlockSpec((None, CONV_WIDTH - 1, C_CONV), lambda b, j: (b, 0, 0))],
        out_shape=[jax.ShapeDtypeStruct((n_seq * seq_len, C_CONV), BF16),
                   jax.ShapeDtypeStruct((n_seq, CONV_WIDTH - 1, C_CONV), F32)],
        scratch_shapes=[pltpu.VMEM((_HALO + tm, C_CONV), F32)],
        compiler_params=_params(("parallel", "arbitrary")),
        name="conv_prompt",
    )(u, w["conv_w"], w["conv_b"], w["conv_ln_g"], w["conv_ln_b"])


def _conv_sample_kernel(u_ref, st_ref, ws_ref, wu_ref, b_ref, g_ref, beta_ref, a_ref, nst_ref):
    n_new = u_ref.shape[0]
    n_st = st_ref.shape[0]
    for t in range(n_new):
        y = jnp.zeros(u_ref.shape[1:], F32) + b_ref[...]
        for r in range(t, n_st):
            y = y + st_ref[r] * ws_ref[t, r:r + 1, :]
        for r in range(t + 1):
            y = y + u_ref[r] * wu_ref[t, r:r + 1, :]
        yn = _ln(y) * g_ref[...] + beta_ref[...]
        a_ref[t] = yn * _sigmoid(yn)
    for r in range(n_st - n_new):
        nst_ref[r] = st_ref[r + n_new]
    for t in range(n_new):
        nst_ref[n_st - n_new + t] = u_ref[t]


def _conv_sample(u_t, st_t, w, nb):
    n_new, n_b, _ = u_t.shape
    n_st = st_t.shape[0]
    return pl.pallas_call(
        _conv_sample_kernel,
        grid=(n_b // nb,),
        in_specs=[pl.BlockSpec((n_new, nb, C_CONV), lambda i: (0, i, 0)),
                  pl.BlockSpec((n_st, nb, C_CONV), lambda i: (0, i, 0)),
                  _const_spec(w["conv_ws"].shape), _const_spec(w["conv_wu"].shape),
                  _const_spec((1, C_CONV)), _const_spec((1, C_CONV)), _const_spec((1, C_CONV))],
        out_specs=[pl.BlockSpec((n_new, nb, C_CONV), lambda i: (0, i, 0)),
                   pl.BlockSpec((n_st, nb, C_CONV), lambda i: (0, i, 0))],
        out_shape=[jax.ShapeDtypeStruct((n_new, n_b, C_CONV), F32),
                   jax.ShapeDtypeStruct((n_st, n_b, C_CONV), F32)],
        compiler_params=_params(("parallel",)),
        name="conv_sample",
    )(u_t, st_t, w["conv_ws"], w["conv_wu"], w["conv_b"], w["conv_ln_g"], w["conv_ln_b"])


def _attn_prompt_kernel(*refs, width, blk, has_bias):
    if has_bias:
        q_ref, k_ref, v_ref, cq_ref, ck_ref, o_ref = refs
    else:
        q_ref, k_ref, v_ref, o_ref = refs
    i = pl.program_id(2)
    lane = lax.broadcasted_iota(jnp.int32, (1, LANES), 1)
    row = lax.broadcasted_iota(jnp.int32, (blk, blk), 0)
    col = lax.broadcasted_iota(jnp.int32, (blk, blk), 1)
    qs, lanes = [], []
    for e in range(2):
        if width == LANES:
            keep = (lane < DH) if e == 0 else (lane >= DH)
            qs.append(jnp.where(keep, q_ref[...], jnp.zeros((), q_ref.dtype)))
            lanes.append(slice(0, LANES))
        else:
            lanes.append(slice(e * LANES, (e + 1) * LANES))
            qs.append(q_ref[:, lanes[e]])

    def block(j, carry, masked):
        start = pl.multiple_of(j * blk, blk)
        v = v_ref[pl.ds(start, blk), :]
        out = []
        for e in range(2):
            m, l, acc = carry[e]
            z = lax.dot_general(qs[e], k_ref[pl.ds(start, blk), lanes[e]], _NT, preferred_element_type=F32)
            if has_bias:
                z = z - ck_ref[j, e:e + 1, :]
            if masked:
                z = jnp.where(col <= row, z, NEG_INF)
            m_new = jnp.maximum(m, jnp.max(z, axis=1, keepdims=True))
            shift = m_new
            if has_bias:
                cq = cq_ref[:, e:e + 1]
                shift = (m_new + cq) - cq
            alpha = jnp.exp(m - m_new)
            p = jnp.exp(z - shift)
            l = alpha * l + jnp.sum(p, axis=1, keepdims=True)
            acc = alpha * acc + _mm(p.astype(BF16), v)
            out.append((m_new, l, acc))
        return tuple(out)

    init = (jnp.full((blk, 1), NEG_INF, F32), jnp.zeros((blk, 1), F32), jnp.zeros((blk, LANES), F32))
    carry = lax.fori_loop(0, i, lambda j, c: block(j, c, False), (init, init))
    (_, l0, acc0), (_, l1, acc1) = block(i, carry, True)
    o_ref[...] = jnp.where(lane < DH, acc0 / l0, acc1 / l1).astype(o_ref.dtype)


def _attn_prompt(q, k, v, bias, n_seq, seq_len, width, blk):
    n_pairs = N_HEADS // 2
    nq = seq_len // blk
    in_specs = [pl.BlockSpec((blk, width), lambda b, p, i: (b * nq + i, p)),
                pl.BlockSpec((seq_len, width), lambda b, p, i: (b, p)),
                pl.BlockSpec((seq_len, LANES), lambda b, p, i: (b, p))]
    args = [q, k, v]
    if bias is not None:
        in_specs += [pl.BlockSpec((None, None, blk, 2), lambda b, p, i: (b, p, i, 0)),
                     pl.BlockSpec((None, None, nq, 2, blk), lambda b, p, i: (b, p, 0, 0, 0))]
        args += list(bias)
    kern = functools.partial(_attn_prompt_kernel, width=width, blk=blk, has_bias=bias is not None)
    return pl.pallas_call(
        kern, grid=(n_seq, n_pairs, nq), in_specs=in_specs,
        out_specs=pl.BlockSpec((blk, LANES), lambda b, p, i: (b * nq + i, p)),
        out_shape=jax.ShapeDtypeStruct((n_seq * seq_len, n_pairs * LANES), BF16),
        compiler_params=_params(("parallel", "parallel", "arbitrary")),
        name="attn_prompt_fox" if bias is not None else "attn_prompt_mla",
    )(*args)


def _softmax_step(m, l, acc, z, pv, crow=None):
    m_new = jnp.maximum(m, jnp.max(z, axis=1, keepdims=True))
    shift = m_new if crow is None else (m_new + crow) - crow
    alpha = jnp.exp(m - m_new)
    p = jnp.exp(z - shift)
    return m_new, alpha * l + jnp.sum(p, axis=1, keepdims=True), alpha * acc + pv(p)


def _fox_dec_kernel(pt_ref, q_ref, kn_ref, vn_ref, cc_ref, *rest, n_pg, n_chunks, n_new):
    k_pages = rest[:n_pg]
    v_pages = rest[n_pg:2 * n_pg]
    f_pages = rest[2 * n_pg:3 * n_pg]
    o_ref = rest[3 * n_pg]
    m_ref, l_ref, acc_ref, tot_ref = rest[3 * n_pg + 1:]
    del pt_ref
    c = pl.program_id(1)
    hd = N_HEADS * DH
    n_rows = n_new * N_HEADS
    lane = lax.broadcasted_iota(jnp.int32, (N_HEADS, hd), 1)
    sub = lax.broadcasted_iota(jnp.int32, (N_HEADS, hd), 0)
    hmask = (lane // DH) == sub
    q_new = q_ref[...]
    q_bd = jnp.concatenate(
        [jnp.where(hmask, jnp.broadcast_to(q_new[t:t + 1, :], (N_HEADS, hd)), 0.0) for t in range(n_new)],
        axis=0)
    ccol = cc_ref[...]
    trow = lax.broadcasted_iota(jnp.int32, (n_rows, 1), 0) // N_HEADS

    @pl.when(c == 0)
    def _():
        tot_ref[...] = jnp.zeros_like(tot_ref)
        m = l = acc = None
        for s_idx in range(n_new):
            k_row = kn_ref[s_idx:s_idx + 1, :]
            v_row = vn_ref[s_idx:s_idx + 1, :]
            c_key = jnp.concatenate([ccol[s_idx * N_HEADS:(s_idx + 1) * N_HEADS, :]] * n_new, axis=0)
            z = jnp.sum(q_bd * k_row, axis=1, keepdims=True) - c_key
            if s_idx == 0:
                m, l, acc = z, jnp.ones_like(z), jnp.broadcast_to(v_row, (n_rows, hd))
            else:
                z = jnp.where(trow >= s_idx, z, NEG_INF)
                m, l, acc = _softmax_step(m, l, acc, z, lambda p, v_row=v_row: p * v_row, ccol)
        m_ref[...] = m
        l_ref[...] = l
        acc_ref[...] = acc

    lf = jnp.concatenate([f_pages[i][...] for i in range(n_pg)], axis=0)
    lane_pg = lax.broadcasted_iota(jnp.int32, lf.shape, 1)
    inc = lf
    d = 1
    while d < LANES:
        inc = inc + jnp.where(lane_pg < LANES - d, pltpu.roll(inc, LANES - d, axis=1), 0.0)
        d *= 2
    exc = inc - lf
    run = tot_ref[...]
    sufs = [None] * n_pg
    for i in reversed(range(n_pg)):
        rows = slice(i * N_HEADS, (i + 1) * N_HEADS)
        sufs[i] = exc[rows, :] + run
        run = run + inc[rows, 0:1]
    tot_ref[...] = run
    suffix = jnp.concatenate(sufs, axis=1)

    qb = q_bd.astype(BF16)
    z = jnp.concatenate([_mm(qb, k_pages[i][...].astype(BF16)) for i in range(n_pg)], axis=1)
    z = z + jnp.concatenate([suffix] * n_new, axis=0)

    def pv(p):
        out = None
        for i in range(n_pg):
            part = lax.dot_general(p[:, i * LANES:(i + 1) * LANES].astype(BF16), v_pages[i][...].astype(BF16),
                                   _NT, preferred_element_type=F32)
            out = part if out is None else out + part
        return out

    m, l, acc = _softmax_step(m_ref[...], l_ref[...], acc_ref[...], z, pv, ccol)
    m_ref[...] = m
    l_ref[...] = l
    acc_ref[...] = acc

    @pl.when(c == n_chunks - 1)
    def _():
        o = acc / l
        for t in range(n_new):
            o_ref[t:t + 1, :] = jnp.sum(jnp.where(hmask, o[t * N_HEADS:(t + 1) * N_HEADS, :], 0.0),
                                        axis=0, keepdims=True)


def _page_specs(layer, block, n_pg, n_chunks):
    def make(i):
        def index_map(b, c, pt):
            return (layer, pt[b, (n_chunks - 1 - c) * n_pg + i]) + (0,) * len(block)
        return pl.BlockSpec((None, None) + block, index_map)
    return [make(i) for i in range(n_pg)]


def _fox_dec(page_table, q, k_new, v_new, ccol, cache_k, cache_v, cache_f, layer, n_pg):
    n_b, n_new, hd = q.shape
    n_pages = page_table.shape[1]
    n_chunks = n_pages // n_pg
    page = cache_k.shape[3]
    n_rows = n_new * N_HEADS
    small = lambda shape: pl.BlockSpec((None,) + shape, lambda b, c, pt: (b,) + (0,) * len(shape))
    in_specs = ([small((n_new, hd)), small((n_new, hd)), small((n_new, hd)), small((n_rows, 1))]
                + _page_specs(layer, (hd, page), n_pg, n_chunks) * 2
                + _page_specs(layer, (N_HEADS, page), n_pg, n_chunks))
    kern = functools.partial(_fox_dec_kernel, n_pg=n_pg, n_chunks=n_chunks, n_new=n_new)
    return pl.pallas_call(
        kern,
        grid_spec=pltpu.PrefetchScalarGridSpec(
            num_scalar_prefetch=1, grid=(n_b, n_chunks), in_specs=in_specs,
            out_specs=pl.BlockSpec((None, n_new, hd), lambda b, c, pt: (b, 0, 0)),
            scratch_shapes=[pltpu.VMEM((n_rows, 1), F32), pltpu.VMEM((n_rows, 1), F32),
                            pltpu.VMEM((n_rows, hd), F32), pltpu.VMEM((N_HEADS, 1), F32)]),
        out_shape=jax.ShapeDtypeStruct((n_b, n_new, hd), F32),
        compiler_params=_params(("parallel", "arbitrary")),
        name="fox_decode",
    )(page_table, q, k_new, v_new, ccol, *([cache_k] * n_pg), *([cache_v] * n_pg), *([cache_f] * n_pg))


def _mla_dec_kernel(pt_ref, qa_ref, qr_ref, ln_ref, rn_ref, *rest, n_pg, n_chunks, n_new):
    l_pages = rest[:n_pg]
    r_pages = rest[n_pg:2 * n_pg]
    o_ref = rest[2 * n_pg]
    m_ref, l_ref, acc_ref = rest[2 * n_pg + 1:]
    del pt_ref
    c = pl.program_id(1)
    n_rows = n_new * N_HEADS
    qa = qa_ref[...]
    qr = qr_ref[...]
    trow = lax.broadcasted_iota(jnp.int32, (n_rows, 1), 0) // N_HEADS

    @pl.when(c == 0)
    def _():
        m = l = acc = None
        for s_idx in range(n_new):
            lat_row = ln_ref[s_idx:s_idx + 1, :]
            kr_row = rn_ref[s_idx:s_idx + 1, :]
            sc = (jnp.sum(qa * lat_row, axis=1, keepdims=True)
                  + jnp.sum(qr * kr_row, axis=1, keepdims=True))
            if s_idx == 0:
                m, l, acc = sc, jnp.ones_like(sc), jnp.broadcast_to(lat_row, (n_rows, KV_LORA))
            else:
                sc = jnp.where(trow >= s_idx, sc, NEG_INF)
                m, l, acc = _softmax_step(m, l, acc, sc, lambda p, lat_row=lat_row: p * lat_row)
        m_ref[...] = m
        l_ref[...] = l
        acc_ref[...] = acc

    qab = qa.astype(BF16)
    qrb = qr.astype(BF16)
    lat = [l_pages[i][...].astype(BF16) for i in range(n_pg)]
    z = jnp.concatenate([lax.dot_general(qab, lat[i], _NT, preferred_element_type=F32)
                         + _mm(qrb, r_pages[i][...].astype(BF16)) for i in range(n_pg)], axis=1)

    def pv(p):
        out = None
        for i in range(n_pg):
            part = _mm(p[:, i * LANES:(i + 1) * LANES].astype(BF16), lat[i])
            out = part if out is None else out + part
        return out

    m, l, acc = _softmax_step(m_ref[...], l_ref[...], acc_ref[...], z, pv)
    m_ref[...] = m
    l_ref[...] = l
    acc_ref[...] = acc

    @pl.when(c == n_chunks - 1)
    def _():
        o_ref[...] = acc / l


def _mla_dec(page_table, qabs, qrope, lat_new, kr_new, cache_lat, cache_kr, layer, n_pg):
    n_b, n_rows, _ = qabs.shape
    n_new = lat_new.shape[1]
    n_pages = page_table.shape[1]
    n_chunks = n_pages // n_pg
    page = cache_lat.shape[2]
    small = lambda shape: pl.BlockSpec((None,) + shape, lambda b, c, pt: (b,) + (0,) * len(shape))
    in_specs = ([small((n_rows, KV_LORA)), small((n_rows, ROPE)), small((n_new, KV_LORA)), small((n_new, ROPE))]
                + _page_specs(layer, (page, KV_LORA), n_pg, n_chunks)
                + _page_specs(layer, (ROPE, page), n_pg, n_chunks))
    kern = functools.partial(_mla_dec_kernel, n_pg=n_pg, n_chunks=n_chunks, n_new=n_new)
    return pl.pallas_call(
        kern,
        grid_spec=pltpu.PrefetchScalarGridSpec(
            num_scalar_prefetch=1, grid=(n_b, n_chunks), in_specs=in_specs,
            out_specs=pl.BlockSpec((None, n_rows, KV_LORA), lambda b, c, pt: (b, 0, 0)),
            scratch_shapes=[pltpu.VMEM((n_rows, 1), F32), pltpu.VMEM((n_rows, 1), F32),
                            pltpu.VMEM((n_rows, KV_LORA), F32)]),
        out_shape=jax.ShapeDtypeStruct((n_b, n_rows, KV_LORA), F32),
        compiler_params=_params(("parallel", "arbitrary")),
        name="mla_decode",
    )(page_table, qabs, qrope, lat_new, kr_new, *([cache_lat] * n_pg), *([cache_kr] * n_pg))


def _merge_kernel(*refs, latent_out, alpha):
    (x_ref, shift_ref, scale_ref, gate_ref, a_ref, of_ref, om_ref, wg_ref, wc_ref, wf_ref) = refs[:10]
    refs = refs[10:]
    if latent_out:
        bduv_ref = refs[0]
        refs = refs[1:]
    wm_ref, wo_ref, g_ref, b_ref, o_ref = refs
    x = x_ref[...]
    h = (_ln(x) * (1.0 + scale_ref[...]) + shift_ref[...]).astype(BF16)
    y_conv = _mm(a_ref[...].astype(BF16), wc_ref[...])
    y_fox = _mm(of_ref[...].astype(BF16), wf_ref[...])
    om = om_ref[...].astype(BF16)
    if latent_out:
        om = _mm(om, bduv_ref[...]).astype(BF16)
    y_mla = _mm(om, wm_ref[...])
    merged = (_sigmoid(_mm(h, wg_ref[:, 0:D_MODEL])) * y_conv
              + _sigmoid(_mm(h, wg_ref[:, D_MODEL:2 * D_MODEL])) * y_fox
              + _sigmoid(_mm(h, wg_ref[:, 2 * D_MODEL:3 * D_MODEL])) * y_mla)
    mix = _mm(merged.astype(BF16), wo_ref[...])
    o_ref[...] = _ln(alpha * x + gate_ref[...] * mix) * g_ref[...] + b_ref[...]


def _merge(x, mod_arr, mod_specs, a, o_fox, o_mla, w, *, latent_out, tm, alpha):
    rows = x.shape[0]
    row_spec = lambda width: pl.BlockSpec((tm, width), lambda i: (i, 0))
    weights = [w["w_gate"], w["conv_w_out"], w["fox_w_o"]] + ([w["bd_uv"]] if latent_out else []) + [
        w["mla_w_o"], w["w_out"], w["ln1_g"], w["ln1_b"]]
    in_specs = ([row_spec(D_MODEL)] + list(mod_specs) + [row_spec(a.shape[1]), row_spec(o_fox.shape[1]),
                                                         row_spec(o_mla.shape[1])]
                + [_const_spec(t.shape) for t in weights])
    kern = functools.partial(_merge_kernel, latent_out=latent_out, alpha=alpha)
    return pl.pallas_call(
        kern, grid=(rows // tm,), in_specs=in_specs, out_specs=row_spec(D_MODEL),
        out_shape=jax.ShapeDtypeStruct((rows, D_MODEL), F32),
        compiler_params=_params(("parallel",)),
        name="merge",
    )(x, mod_arr, mod_arr, mod_arr, a, o_fox, o_mla, *weights)


def _ffn_kernel(x_ref, shift_ref, scale_ref, gate_ref, wu_ref, wd_ref, g_ref, b_ref, o_ref, *, alpha, chunk):
    x = x_ref[...]
    h = (_ln(x) * (1.0 + scale_ref[...]) + shift_ref[...]).astype(BF16)
    ff = jnp.zeros(x.shape, F32)
    for s in range(0, D_FF, chunk):
        t = jnp.maximum(_mm(h, wu_ref[:, s:s + chunk]), 0.0)
        ff = ff + _mm((t * t).astype(BF16), wd_ref[s:s + chunk, :])
    o_ref[...] = _ln(alpha * x + gate_ref[...] * ff) * g_ref[...] + b_ref[...]


def _ffn(x, mod_arr, mod_specs, w, *, tm, alpha):
    rows = x.shape[0]
    row_spec = pl.BlockSpec((tm, D_MODEL), lambda i: (i, 0))
    weights = [w["w_up"], w["w_down"], w["ln2_g"], w["ln2_b"]]
    kern = functools.partial(_ffn_kernel, alpha=alpha, chunk=1024)
    return pl.pallas_call(
        kern, grid=(rows // tm,), in_specs=[row_spec] + list(mod_specs) + [_const_spec(t.shape) for t in weights],
        out_specs=row_spec, out_shape=jax.ShapeDtypeStruct((rows, D_MODEL), F32),
        compiler_params=_params(("parallel",)),
        name="ffn",
    )(x, mod_arr, mod_arr, mod_arr, *weights)


def _swap_halves(w):
    half = w.shape[-1] // 2
    return jnp.concatenate([w[..., half:], w[..., :half]], axis=-1)


def _block_diag(blocks):
    n, r, c = blocks.shape
    eye = jnp.eye(n, dtype=blocks.dtype)
    return (eye[:, None, :, None] * blocks[:, :, None, :]).reshape(n * r, n * c)


def _layer_weights(p, n_new):
    w_in = p["w_in"]
    hd = N_HEADS * DH
    o_f = 2 * C_CONV + 3 * hd
    o_ql = o_f + N_HEADS
    o_kv = o_ql + Q_LORA
    o_kr = o_kv + KV_LORA
    o_g = o_kr + ROPE
    w_f = w_in[:, o_f:o_ql]
    w_kr = w_in[:, o_kr:o_g]
    w_krr = _swap_halves(w_kr)
    zeros = lambda n: jnp.zeros((D_MODEL, n), F32)
    w = {}
    w["w_main"] = w_in[:, :o_f].astype(BF16)
    w["w_ql"] = w_in[:, o_ql:o_kv].astype(BF16)
    w["w_kv"] = w_in[:, o_kv:o_kr].astype(BF16)
    w["w_s"] = jnp.concatenate([w_kr, w_krr, w_f, zeros(LANES - 2 * ROPE - N_HEADS)], axis=1).astype(BF16)
    w["b_f"] = jnp.zeros((1, LANES), F32).at[0, 2 * ROPE:2 * ROPE + N_HEADS].set(p["b_f"])
    w["g_q"] = p["mla_q_norm_g"].reshape(1, Q_LORA)
    w["g_kv"] = p["mla_kv_norm_g"].reshape(1, KV_LORA)
    w["w_gate"] = w_in[:, o_g:].astype(BF16)
    w["w_kr"] = jnp.concatenate([zeros(DH), w_kr, zeros(HEAD_PAD - DH - ROPE)], axis=1).astype(BF16)
    w["w_krr"] = jnp.concatenate([zeros(DH), w_krr, zeros(HEAD_PAD - DH - ROPE)], axis=1).astype(BF16)
    w_qb = p["mla_w_qb"].reshape(Q_LORA, N_HEADS, DH + ROPE)
    qz = lambda n: jnp.zeros((Q_LORA, N_HEADS, n), F32)
    w["w_qa"] = jnp.concatenate([w_qb, qz(HEAD_PAD - DH - ROPE)], axis=2).reshape(Q_LORA, -1).astype(BF16)
    w["w_qb"] = jnp.concatenate([qz(DH), _swap_halves(w_qb[..., DH:]), qz(HEAD_PAD - DH - ROPE)],
                                axis=2).reshape(Q_LORA, -1).astype(BF16)
    w_uk = p["mla_w_uk"]
    w["w_ka"] = jnp.concatenate([w_uk, jnp.zeros((KV_LORA, N_HEADS, HEAD_PAD - DH), F32)],
                                axis=2).reshape(KV_LORA, -1).astype(BF16)
    w["w_uv"] = p["mla_w_uv"].reshape(KV_LORA, hd).astype(BF16)
    w["w_qn"] = w_qb[..., :DH].reshape(Q_LORA, hd).astype(BF16)
    w["w_qr"] = w_qb[..., DH:].reshape(Q_LORA, N_HEADS * ROPE).astype(BF16)
    w["w_qrr"] = _swap_halves(w_qb[..., DH:]).reshape(Q_LORA, N_HEADS * ROPE).astype(BF16)
    w["bd_uk"] = _block_diag(jnp.transpose(w_uk, (1, 2, 0))).astype(BF16)
    w["bd_uv"] = _block_diag(jnp.transpose(p["mla_w_uv"], (1, 0, 2))).astype(BF16)
    cw = p["conv_w"]
    w["conv_w"] = jnp.concatenate([cw, jnp.zeros((_HALO - CONV_WIDTH, C_CONV), F32)], axis=0)
    n_st = CONV_WIDTH - 1
    idx = jnp.arange(n_st + n_new)[None, :] - jnp.arange(n_new)[:, None]
    taps = jnp.where(((idx >= 0) & (idx < CONV_WIDTH))[..., None], cw[jnp.clip(idx, 0, CONV_WIDTH - 1)], 0.0)
    w["conv_ws"] = taps[:, :n_st]
    w["conv_wu"] = taps[:, n_st:]
    for name in ("conv_b", "conv_ln_g", "conv_ln_b"):
        w[name] = p[name].reshape(1, C_CONV)
    for name in ("conv_w_out", "fox_w_o", "mla_w_o", "w_out", "w_up", "w_down"):
        w[name] = p[name].astype(BF16)
    for name in ("ln1_g", "ln1_b", "ln2_g", "ln2_b"):
        w[name] = p[name].reshape(1, D_MODEL)
    return w


def _rope_tables(pos, reps):
    half = ROPE // 2
    inv_freq = ROPE_BASE ** (-jnp.arange(half, dtype=F32) / half)
    ang = pos.astype(F32)[:, None] * inv_freq
    cos, sin = jnp.cos(ang), jnp.sin(ang)
    cos32 = jnp.concatenate([cos, cos], axis=1)
    sin32 = jnp.concatenate([-sin, sin], axis=1)
    n = pos.shape[0]
    tabs = {"cos32": cos32, "sin32": sin32,
            "cos128": jnp.concatenate([jnp.ones((n, DH), F32), cos32, jnp.zeros((n, HEAD_PAD - DH - ROPE), F32)], 1),
            "sin128": jnp.concatenate([jnp.zeros((n, DH), F32), sin32, jnp.zeros((n, HEAD_PAD - DH - ROPE), F32)], 1),
            "cos256": jnp.tile(cos32, (1, N_HEADS)), "sin256": jnp.tile(sin32, (1, N_HEADS))}
    if reps > 1:
        tabs = {k: jnp.tile(v, (reps, 1)) for k, v in tabs.items()}
    return tabs


def _mod_specs_seq(cols, tiles_per_seq):
    return [pl.BlockSpec((None, 1, D_MODEL), functools.partial(
        lambda i, c: (i // tiles_per_seq, 0, c), c=c)) for c in cols]


def _mod_specs_rows(cols, tm):
    return [pl.BlockSpec((tm, D_MODEL), functools.partial(lambda i, c: (i, c), c=c)) for c in cols]


def kernel(x_prompt, x_sample, cache_fox_k, cache_fox_v, cache_fox_logf, cache_mla_latent, cache_mla_krope,
           state_conv, page_table, c_prompt, c_sample, w_ada, b_ada, w_in, b_f, conv_w, conv_b, conv_ln_g,
           conv_ln_b, conv_w_out, fox_w_o, mla_q_norm_g, mla_w_qb, mla_kv_norm_g, mla_w_uk, mla_w_uv, mla_w_o,
           w_out, ln1_g, ln1_b, w_up, w_down, ln2_g, ln2_b):
    n_layers = w_in.shape[0]
    n_b, seq_len, _ = x_prompt.shape
    n_s, n_new, _ = x_sample.shape
    page = cache_fox_k.shape[2]
    n_pages = page_table.shape[1]
    n_past = n_pages * page
    alpha = float((2 * n_layers) ** 0.25)
    hd = N_HEADS * DH
    n_pairs = N_HEADS // 2

    tm_p = min(512, seq_len)
    rows_s = n_s * n_new
    tm_s = min(256, rows_s)
    blk = min(512, seq_len)
    n_pg = min(16, n_pages)
    nb_conv = min(8, n_s)

    n_pool = cache_fox_k.shape[1]
    ck = jnp.transpose(cache_fox_k, (0, 1, 3, 4, 2)).reshape(n_layers, n_pool, hd, page)
    cv = jnp.transpose(cache_fox_v, (0, 1, 3, 4, 2)).reshape(n_layers, n_pool, hd, page)
    cf = jnp.transpose(cache_fox_logf, (0, 1, 3, 2))
    ckr = jnp.transpose(cache_mla_krope, (0, 1, 3, 2))

    m_all = n_b + n_s
    m_pad = -(-m_all // 8) * 8
    c_all = jnp.concatenate([c_prompt, c_sample, jnp.zeros((m_pad - m_all, D_MODEL), F32)], axis=0)
    mod_all = _ada(c_all, w_ada, b_ada)

    tabs_p = _rope_tables(jnp.arange(seq_len, dtype=jnp.int32), 1)
    tabs_s = _rope_tables(n_past + jnp.arange(n_new, dtype=jnp.int32), n_s)

    params = dict(w_in=w_in, b_f=b_f, conv_w=conv_w, conv_b=conv_b, conv_ln_g=conv_ln_g, conv_ln_b=conv_ln_b,
                  conv_w_out=conv_w_out, fox_w_o=fox_w_o, mla_q_norm_g=mla_q_norm_g, mla_w_qb=mla_w_qb,
                  mla_kv_norm_g=mla_kv_norm_g, mla_w_uk=mla_w_uk, mla_w_uv=mla_w_uv, mla_w_o=mla_w_o,
                  w_out=w_out, ln1_g=ln1_g, ln1_b=ln1_b, w_up=w_up, w_down=w_down, ln2_g=ln2_g, ln2_b=ln2_b)

    xp = x_prompt.reshape(n_b * seq_len, D_MODEL)
    xs = x_sample.reshape(rows_s, D_MODEL)
    rows_p, rows_smp = [], []
    tps = seq_len // tm_p
    for l in range(n_layers):
        w = _layer_weights({k: v[l] for k, v in params.items()}, n_new)
        mod_p = mod_all[l, :n_b].reshape(n_b, 1, 6 * D_MODEL)
        mod_s = jnp.repeat(mod_all[l, n_b:n_b + n_s], n_new, axis=0)

        (u, kf, vf, logf, cs, lat, kr, qf, kb, vb, mq, mk, mv) = _inproj(
            xp, mod_p, _mod_specs_seq((0, 1), tps), w, tabs_p, prompt=True, seg=seq_len, tm=tm_p)
        a_p, st_p = _conv_prompt(u, w, n_b, seq_len, tm_p)
        c4 = cs.reshape(n_b, seq_len, n_pairs, 2)
        cq = jnp.transpose(c4, (0, 2, 1, 3))
        ckk = jnp.transpose(c4.reshape(n_b, seq_len // blk, blk, n_pairs, 2), (0, 3, 1, 4, 2))
        o_fox = _attn_prompt(qf, kb, vb, (cq, ckk), n_b, seq_len, LANES, blk)
        o_mla = _attn_prompt(mq, mk, mv, None, n_b, seq_len, 2 * HEAD_PAD, blk)
        x1 = _merge(xp, mod_p, _mod_specs_seq((0, 1, 2), tps), a_p, o_fox, o_mla, w,
                    latent_out=False, tm=tm_p, alpha=alpha)
        xp = _ffn(x1, mod_p, _mod_specs_seq((3, 4, 5), tps), w, tm=tm_p, alpha=alpha)
        rows_p.append((kf.reshape(n_b, seq_len, N_HEADS, DH), vf.reshape(n_b, seq_len, N_HEADS, DH),
                       logf.reshape(n_b, seq_len, N_HEADS), lat.reshape(n_b, seq_len, KV_LORA),
                       kr.reshape(n_b, seq_len, ROPE), st_p))

        (u, kf, vf, logf, cs, lat, kr, qf, qabs, qrope) = _inproj(
            xs, mod_s, _mod_specs_rows((0, 1), tm_s), w, tabs_s, prompt=False, seg=n_new, tm=tm_s)
        u_t = jnp.transpose(u.reshape(n_s, n_new, C_CONV), (1, 0, 2))
        st_t = jnp.transpose(state_conv[l], (1, 0, 2))
        a_t, nst_t = _conv_sample(u_t, st_t, w, nb_conv)
        a_s = jnp.transpose(a_t, (1, 0, 2)).reshape(rows_s, C_CONV)
        st_s = jnp.transpose(nst_t, (1, 0, 2))
        o_fox = _fox_dec(page_table, qf.reshape(n_s, n_new, hd), kf.reshape(n_s, n_new, hd),
                         vf.reshape(n_s, n_new, hd), cs.reshape(n_s, n_new * N_HEADS, 1),
                         ck, cv, cf, l, n_pg)
        o_lat = _mla_dec(page_table, qabs.reshape(n_s, n_new * N_HEADS, KV_LORA),
                         qrope.reshape(n_s, n_new * N_HEADS, ROPE), lat.reshape(n_s, n_new, KV_LORA),
                         kr.reshape(n_s, n_new, ROPE), cache_mla_latent, ckr, l, n_pg)
        x1 = _merge(xs, mod_s, _mod_specs_rows((0, 1, 2), tm_s), a_s, o_fox.reshape(rows_s, hd),
                    o_lat.reshape(rows_s, N_HEADS * KV_LORA), w, latent_out=True, tm=tm_s, alpha=alpha)
        xs = _ffn(x1, mod_s, _mod_specs_rows((3, 4, 5), tm_s), w, tm=tm_s, alpha=alpha)
        rows_smp.append((kf.reshape(n_s, n_new, N_HEADS, DH), vf.reshape(n_s, n_new, N_HEADS, DH),
                         logf.reshape(n_s, n_new, N_HEADS), lat.reshape(n_s, n_new, KV_LORA),
                         kr.reshape(n_s, n_new, ROPE), st_s))

    def stk(rows, i):
        return jnp.stack([r[i] for r in rows])

    return ((xp.reshape(n_b, seq_len, D_MODEL), xs.reshape(n_s, n_new, D_MODEL))
            + tuple(stk(rows_p, i) for i in range(6)) + tuple(stk(rows_smp, i) for i in range(6)))
```

```python
import functools
import math

import jax
import jax.numpy as jnp
from jax import lax
from jax.experimental import pallas as pl
from jax.experimental.pallas import tpu as pltpu

F32 = jnp.float32
BF16 = jnp.bfloat16

D_MODEL = 1024
C_CONV = D_MODEL // 2
CONV_WIDTH = 31
N_HEADS = 8
DH = D_MODEL // 16
ROPE = D_MODEL // 32
KV_LORA = D_MODEL // 4
Q_LORA = 3 * KV_LORA
D_FF = 4 * D_MODEL
ROPE_BASE = 10000.0
NEG_INF = -1e30
LOG2E = math.log2(math.e)
LANES = 128
HEAD_PAD = 128
VMEM_LIMIT = 56 * 1024 * 1024

_NT = (((1,), (1,)), ((), ()))


def _params(sem, vmem=VMEM_LIMIT):
    return pltpu.CompilerParams(dimension_semantics=sem, vmem_limit_bytes=vmem)


def _const_spec(shape):
    nd = len(shape)
    return pl.BlockSpec(shape, lambda *_: (0,) * nd, pipeline_mode=pl.Buffered(1))


def _ln(x, eps=1e-5):
    mu = jnp.mean(x, axis=-1, keepdims=True)
    xc = x - mu
    var = jnp.mean(xc * xc, axis=-1, keepdims=True)
    return xc * lax.rsqrt(var + eps)


def _rms(x, eps=1e-6):
    return x * lax.rsqrt(jnp.mean(x * x, axis=-1, keepdims=True) + eps)


def _sigmoid(x):
    return 1.0 / (1.0 + jnp.exp(-x))


def _log_sigmoid(x):
    return jnp.minimum(x, 0.0) - jnp.log(1.0 + jnp.exp(-jnp.abs(x)))


def _mm(a, b):
    return jnp.dot(a, b, preferred_element_type=F32)


def _ada_kernel(c_ref, w_ref, b_ref, o_ref):
    c = c_ref[...]
    a = (c * _sigmoid(c)).astype(BF16)
    o_ref[...] = _mm(a, w_ref[...].astype(BF16)) + b_ref[...]


def _ada(c_all, w_ada, b_ada):
    n_layers, _, n_out = w_ada.shape
    m = c_all.shape[0]
    tn = 1536
    return pl.pallas_call(
        _ada_kernel,
        grid=(n_layers, n_out // tn),
        in_specs=[pl.BlockSpec((m, D_MODEL), lambda l, j: (0, 0)),
                  pl.BlockSpec((None, D_MODEL, tn), lambda l, j: (l, 0, j)),
                  pl.BlockSpec((None, 1, tn), lambda l, j: (l, 0, j))],
        out_specs=pl.BlockSpec((None, m, tn), lambda l, j: (l, 0, j)),
        out_shape=jax.ShapeDtypeStruct((n_layers, m, n_out), F32),
        compiler_params=_params(("parallel", "parallel")),
        name="ada_mod",
    )(c_all, w_ada, b_ada.reshape(n_layers, 1, n_out))


def _seg_cumsum(x, seg):
    rows = x.shape[0]
    row = lax.broadcasted_iota(jnp.int32, x.shape, 0)
    rin = row % seg if seg < rows else row
    d = 1
    while d < min(seg, rows):
        x = x + jnp.where(rin >= d, pltpu.roll(x, d, axis=0), 0.0)
        d *= 2
    return x


def _inproj_kernel(*refs, prompt, seg, tiles_per_seq, fox_scale, mla_scale):
    (x_ref, shift_ref, scale_ref, wmain_ref, wql_ref, wkv_ref, ws_ref, bf_ref, gq_ref, gkv_ref,
     cos32_ref, sin32_ref) = refs[:12]
    refs = refs[12:]
    if prompt:
        (wkr_ref, wkrr_ref, cos128_ref, sin128_ref, wqa_ref, wqb_ref, wka_ref, wuv_ref) = refs[:8]
        refs = refs[8:]
        (u_ref, kf_ref, vf_ref, logf_ref, c_ref, lat_ref, kr_ref,
         qf_ref, kb_ref, vb_ref, mq_ref, mk_ref, mv_ref, carry_ref) = refs
    else:
        (wqn_ref, bduk_ref, wqr_ref, wqrr_ref, cos256_ref, sin256_ref) = refs[:6]
        refs = refs[6:]
        (u_ref, kf_ref, vf_ref, logf_ref, c_ref, lat_ref, kr_ref,
         qf_ref, qabs_ref, qrope_ref, carry_ref) = refs

    i = pl.program_id(0)
    x = x_ref[...]
    tm = x.shape[0]
    h = (_ln(x) * (1.0 + scale_ref[...]) + shift_ref[...]).astype(BF16)

    u_ref[...] = _mm(h, wmain_ref[:, 0:C_CONV]) * _sigmoid(_mm(h, wmain_ref[:, C_CONV:2 * C_CONV]))
    o = 2 * C_CONV
    hd = N_HEADS * DH
    qf_ref[...] = (_mm(h, wmain_ref[:, o:o + hd]) * fox_scale).astype(qf_ref.dtype)
    kf = _mm(h, wmain_ref[:, o + hd:o + 2 * hd])
    vf = _mm(h, wmain_ref[:, o + 2 * hd:o + 3 * hd])
    kf_ref[...] = kf
    vf_ref[...] = vf
    if prompt:
        kb_ref[...] = kf.astype(BF16)
        vb_ref[...] = vf.astype(BF16)

    zs = _mm(h, ws_ref[...])
    kr_ref[...] = zs[:, 0:ROPE] * cos32_ref[...] + zs[:, ROPE:2 * ROPE] * sin32_ref[...]
    logf = _log_sigmoid(zs + bf_ref[...])
    logf_ref[...] = logf[:, 2 * ROPE:2 * ROPE + N_HEADS]
    cs = _seg_cumsum(logf, seg)
    if seg >= tm:
        @pl.when(i % tiles_per_seq == 0)
        def _():
            carry_ref[...] = jnp.zeros_like(carry_ref)
        cs = cs + carry_ref[...]
        carry_ref[...] = cs[tm - 1:tm, :]
    c_ref[...] = cs[:, 2 * ROPE:2 * ROPE + N_HEADS]

    latn = _rms(_mm(h, wkv_ref[...])) * gkv_ref[...]
    lat_ref[...] = latn
    qn = (_rms(_mm(h, wql_ref[...])) * gq_ref[...]).astype(BF16)
    if prompt:
        latb = latn.astype(BF16)
        cosq = cos128_ref[...]
        sinq = sin128_ref[...]
        krt = _mm(h, wkr_ref[...]) * cosq + _mm(h, wkrr_ref[...]) * sinq
        per_head = lambda t: jnp.concatenate([t] * N_HEADS, axis=1)
        mq_ref[...] = ((_mm(qn, wqa_ref[...]) * per_head(cosq) + _mm(qn, wqb_ref[...]) * per_head(sinq))
                       * mla_scale).astype(BF16)
        mk_ref[...] = (_mm(latb, wka_ref[...]) + per_head(krt)).astype(BF16)
        mv_ref[...] = _mm(latb, wuv_ref[...]).astype(BF16)
    else:
        qnope = _mm(qn, wqn_ref[...]).astype(BF16)
        qabs_ref[...] = _mm(qnope, bduk_ref[...]) * mla_scale
        qrope_ref[...] = (_mm(qn, wqr_ref[...]) * cos256_ref[...]
                          + _mm(qn, wqrr_ref[...]) * sin256_ref[...]) * mla_scale


def _inproj(x, shift_spec_arr, mod_specs, w, tabs, *, prompt, seg, tm):
    rows = x.shape[0]
    n_tiles = rows // tm
    n_tab = tabs["cos32"].shape[0] // tm
    tiles_per_seq = max(seg // tm, 1)
    exp_scale = LOG2E if prompt else 1.0

    def row_spec(width):
        return pl.BlockSpec((tm, width), lambda i: (i, 0))

    def tab_spec(width):
        return pl.BlockSpec((tm, width), lambda i: (i % n_tab, 0))

    common_w = [w["w_main"], w["w_ql"], w["w_kv"], w["w_s"], w["b_f"], w["g_q"], w["g_kv"]]
    in_arrays = [x, shift_spec_arr, shift_spec_arr] + common_w + [tabs["cos32"], tabs["sin32"]]
    in_specs = ([row_spec(D_MODEL), mod_specs[0], mod_specs[1]]
                + [_const_spec(a.shape) for a in common_w] + [tab_spec(ROPE), tab_spec(ROPE)])
    hd = N_HEADS * DH
    out_shapes = [jax.ShapeDtypeStruct((rows, C_CONV), F32),
                  jax.ShapeDtypeStruct((rows, hd), F32),
                  jax.ShapeDtypeStruct((rows, hd), F32),
                  jax.ShapeDtypeStruct((rows, N_HEADS), F32),
                  jax.ShapeDtypeStruct((rows, N_HEADS), F32),
                  jax.ShapeDtypeStruct((rows, KV_LORA), F32),
                  jax.ShapeDtypeStruct((rows, ROPE), F32)]
    out_specs = [row_spec(C_CONV), row_spec(hd), row_spec(hd), row_spec(N_HEADS), row_spec(N_HEADS),
                 row_spec(KV_LORA), row_spec(ROPE)]
    if prompt:
        extra_w = [w["w_kr"], w["w_krr"]]
        in_arrays += extra_w + [tabs["cos128"], tabs["sin128"]]
        in_specs += [_const_spec(a.shape) for a in extra_w] + [tab_spec(LANES), tab_spec(LANES)]
        extra_w2 = [w["w_qa"], w["w_qb"], w["w_ka"], w["w_uv"]]
        in_arrays += extra_w2
        in_specs += [_const_spec(a.shape) for a in extra_w2]
        wide = N_HEADS * HEAD_PAD
        out_shapes += [jax.ShapeDtypeStruct((rows, hd), BF16)] * 3 + [
            jax.ShapeDtypeStruct((rows, wide), BF16), jax.ShapeDtypeStruct((rows, wide), BF16),
            jax.ShapeDtypeStruct((rows, hd), BF16)]
        out_specs += [row_spec(hd)] * 3 + [row_spec(wide), row_spec(wide), row_spec(hd)]
    else:
        extra_w = [w["w_qn"], w["bd_uk"], w["w_qr"], w["w_qrr"]]
        in_arrays += extra_w + [tabs["cos256"], tabs["sin256"]]
        in_specs += [_const_spec(a.shape) for a in extra_w] + [tab_spec(N_HEADS * ROPE)] * 2
        out_shapes += [jax.ShapeDtypeStruct((rows, hd), F32),
                       jax.ShapeDtypeStruct((rows, N_HEADS * KV_LORA), F32),
                       jax.ShapeDtypeStruct((rows, N_HEADS * ROPE), F32)]
        out_specs += [row_spec(hd), row_spec(N_HEADS * KV_LORA), row_spec(N_HEADS * ROPE)]

    kern = functools.partial(_inproj_kernel, prompt=prompt, seg=seg, tiles_per_seq=tiles_per_seq,
                             fox_scale=float(DH ** -0.5) * exp_scale,
                             mla_scale=float((DH + ROPE) ** -0.5) * exp_scale)
    return pl.pallas_call(
        kern, grid=(n_tiles,), in_specs=in_specs, out_specs=out_specs, out_shape=out_shapes,
        scratch_shapes=[pltpu.VMEM((1, LANES), F32)],
        compiler_params=_params(("arbitrary",)),
        name="inproj_prompt" if prompt else "inproj_sample",
    )(*in_arrays)


_HALO = 32


_SUBLANES = 8


def _conv_prompt_kernel(u_ref, w_ref, b_ref, g_ref, beta_ref, a_ref, st_ref, ext_ref, sh_ref):
    j = pl.program_id(1)
    tm = u_ref.shape[0]

    @pl.when(j == 0)
    def _():
        ext_ref[0:_HALO, :] = jnp.zeros((_HALO, C_CONV), F32)

    ext_ref[_HALO:_HALO + tm, :] = u_ref[...]
    off = _HALO - (CONV_WIDTH - 1)
    n_sh = sh_ref.shape[1]
    for r in range(1, _SUBLANES):
        sh_ref[r - 1] = ext_ref[r:r + n_sh, :]
    y = jnp.zeros((tm, C_CONV), F32) + b_ref[...]
    for k in range(CONV_WIDTH):
        q, r = divmod(off + k, _SUBLANES)
        win = ext_ref[q * _SUBLANES:q * _SUBLANES + tm, :] if r == 0 else sh_ref[r - 1, q * _SUBLANES:q * _SUBLANES + tm, :]
        y = y + win * w_ref[k:k + 1, :]
    yn = _ln(y) * g_ref[...] + beta_ref[...]
    a_ref[...] = (yn * _sigmoid(yn)).astype(a_ref.dtype)
    st_ref[...] = ext_ref[tm + off:tm + _HALO, :]
    ext_ref[0:_HALO, :] = ext_ref[tm:tm + _HALO, :]


def _conv_prompt(u, w, n_seq, seq_len, tm):
    nt = seq_len // tm
    return pl.pallas_call(
        _conv_prompt_kernel,
        grid=(n_seq, nt),
        in_specs=[pl.BlockSpec((tm, C_CONV), lambda b, j: (b * nt + j, 0)),
                  _const_spec(w["conv_w"].shape), _const_spec((1, C_CONV)), _const_spec((1, C_CONV)),
                  _const_spec((1, C_CONV))],
        out_specs=[pl.BlockSpec((tm, C_CONV), lambda b, j: (b * nt + j, 0)),
                   pl.BlockSpec((None, CONV_WIDTH - 1, C_CONV), lambda b, j: (b, 0, 0))],
        out_shape=[jax.ShapeDtypeStruct((n_seq * seq_len, C_CONV), BF16),
                   jax.ShapeDtypeStruct((n_seq, CONV_WIDTH - 1, C_CONV), F32)],
        scratch_shapes=[pltpu.VMEM((_HALO + tm, C_CONV), F32),
                        pltpu.VMEM((_SUBLANES - 1, _HALO - _SUBLANES + tm, C_CONV), F32)],
        compiler_params=_params(("parallel", "arbitrary")),
        name="conv_prompt",
    )(u, w["conv_w"], w["conv_b"], w["conv_ln_g"], w["conv_ln_b"])


def _conv_sample_kernel(u_ref, st_ref, ws_ref, wu_ref, b_ref, g_ref, beta_ref, a_ref, nst_ref):
    n_new = u_ref.shape[0]
    n_st = st_ref.shape[0]
    for t in range(n_new):
        y = jnp.zeros(u_ref.shape[1:], F32) + b_ref[...]
        for r in range(t, n_st):
            y = y + st_ref[r] * ws_ref[t, r:r + 1, :]
        for r in range(t + 1):
            y = y + u_ref[r] * wu_ref[t, r:r + 1, :]
        yn = _ln(y) * g_ref[...] + beta_ref[...]
        a_ref[t] = yn * _sigmoid(yn)
    for r in range(n_st - n_new):
        nst_ref[r] = st_ref[r + n_new]
    for t in range(n_new):
        nst_ref[n_st - n_new + t] = u_ref[t]


def _conv_sample(u_t, st_t, w, nb):
    n_new, n_b, _ = u_t.shape
    n_st = st_t.shape[0]
    return pl.pallas_call(
        _conv_sample_kernel,
        grid=(n_b // nb,),
        in_specs=[pl.BlockSpec((n_new, nb, C_CONV), lambda i: (0, i, 0)),
                  pl.BlockSpec((n_st, nb, C_CONV), lambda i: (0, i, 0)),
                  _const_spec(w["conv_ws"].shape), _const_spec(w["conv_wu"].shape),
                  _const_spec((1, C_CONV)), _const_spec((1, C_CONV)), _const_spec((1, C_CONV))],
        out_specs=[pl.BlockSpec((n_new, nb, C_CONV), lambda i: (0, i, 0)),
                   pl.BlockSpec((n_st, nb, C_CONV), lambda i: (0, i, 0))],
        out_shape=[jax.ShapeDtypeStruct((n_new, n_b, C_CONV), F32),
                   jax.ShapeDtypeStruct((n_st, n_b, C_CONV), F32)],
        compiler_params=_params(("parallel",)),
        name="conv_sample",
    )(u_t, st_t, w["conv_ws"], w["conv_wu"], w["conv_b"], w["conv_ln_g"], w["conv_ln_b"])


def _attn_prompt_kernel(*refs, width, blk, has_bias, n_sub, n_q):
    if has_bias:
        q_ref, k_ref, v_ref, cq_ref, ck_ref, o_ref = refs
    else:
        q_ref, k_ref, v_ref, o_ref = refs
    i = pl.program_id(2)
    lane = lax.broadcasted_iota(jnp.int32, (1, LANES), 1)
    sub = blk // n_sub
    row = lax.broadcasted_iota(jnp.int32, (sub, blk), 0)
    col = lax.broadcasted_iota(jnp.int32, (sub, blk), 1)
    qs, lanes = [], []
    for e in range(2):
        if width == LANES:
            keep = (lane < DH) if e == 0 else (lane >= DH)
            qs.append(jnp.where(keep, q_ref[...], jnp.zeros((), q_ref.dtype)))
            lanes.append(slice(0, LANES))
        else:
            lanes.append(slice(e * LANES, (e + 1) * LANES))
            qs.append(q_ref[:, lanes[e]])

    def block(j, carry, masked):
        start = j * blk
        v = v_ref[pl.ds(start, blk), :]
        out = []
        for e in range(2):
            k = k_ref[pl.ds(start, blk), lanes[e]]
            for sb in range(n_sub):
                rows = slice(sb * sub, (sb + 1) * sub)
                m, l, acc = carry[e * n_sub + sb]
                z = lax.dot_general(qs[e][rows, :], k, _NT, preferred_element_type=F32)
                if has_bias:
                    z = z - ck_ref[j, e:e + 1, :] * LOG2E
                if masked:
                    z = jnp.where(col <= row + sb * sub, z, NEG_INF)
                m_new = jnp.maximum(m, jnp.max(z, axis=1, keepdims=True))
                shift = m_new
                if has_bias:
                    cq = cq_ref[rows, e:e + 1] * LOG2E
                    shift = (m_new + cq) - cq
                alpha = jnp.exp2(m - m_new)
                p = jnp.exp2(z - shift)
                l = alpha * l + jnp.sum(p, axis=1, keepdims=True)
                acc = alpha * acc + _mm(p.astype(BF16), v)
                out.append((m_new, l, acc))
        return tuple(out)

    init = (jnp.full((sub, 1), NEG_INF, F32), jnp.zeros((sub, 1), F32), jnp.zeros((sub, LANES), F32))
    for iv in range(n_q):
        @pl.when(i == iv)
        def _(iv=iv):
            carry = (init,) * (2 * n_sub)
            for j in range(iv):
                carry = block(j, carry, False)
            res = block(iv, carry, True)
            for sb in range(n_sub):
                (_, l0, acc0), (_, l1, acc1) = res[sb], res[n_sub + sb]
                o_ref[sb * sub:(sb + 1) * sub, :] = jnp.where(lane < DH, acc0 / l0, acc1 / l1).astype(o_ref.dtype)


def _attn_prompt(q, k, v, bias, n_seq, seq_len, width, blk):
    n_pairs = N_HEADS // 2
    nq = seq_len // blk
    in_specs = [pl.BlockSpec((blk, width), lambda b, p, i: (b * nq + i, p)),
                pl.BlockSpec((seq_len, width), lambda b, p, i: (b, p)),
                pl.BlockSpec((seq_len, LANES), lambda b, p, i: (b, p))]
    args = [q, k, v]
    if bias is not None:
        in_specs += [pl.BlockSpec((None, None, blk, 2), lambda b, p, i: (b, p, i, 0)),
                     pl.BlockSpec((None, None, nq, 2, blk), lambda b, p, i: (b, p, 0, 0, 0))]
        args += list(bias)
    kern = functools.partial(_attn_prompt_kernel, width=width, blk=blk, has_bias=bias is not None, n_sub=1,
                             n_q=nq)
    return pl.pallas_call(
        kern, grid=(n_seq, n_pairs, nq), in_specs=in_specs,
        out_specs=pl.BlockSpec((blk, LANES), lambda b, p, i: (b * nq + i, p)),
        out_shape=jax.ShapeDtypeStruct((n_seq * seq_len, n_pairs * LANES), BF16),
        compiler_params=_params(("parallel", "parallel", "arbitrary")),
        name="attn_prompt_fox" if bias is not None else "attn_prompt_mla",
    )(*args)


def _softmax_step(m, l, acc, z, pv, crow=None):
    m_new = jnp.maximum(m, jnp.max(z, axis=1, keepdims=True))
    shift = m_new if crow is None else (m_new + crow) - crow
    alpha = jnp.exp(m - m_new)
    p = jnp.exp(z - shift)
    return m_new, alpha * l + jnp.sum(p, axis=1, keepdims=True), alpha * acc + pv(p)


def _chunk_copies(pt_ref, caches, bufs, sems, layer, b, c, slot, n_pg, n_chunks, read_table):
    first = (n_chunks - 1 - c) * n_pg
    out = []
    for i in range(n_pg):
        page = pt_ref[b, first + i] if read_table else 0
        for a, (cache, buf) in enumerate(zip(caches, bufs)):
            out.append(pltpu.make_async_copy(cache.at[layer, page], buf.at[slot, i], sems.at[slot, a]))
    return out


def _page_pipeline(pt_ref, caches, bufs, sems, layer, n_pg, n_chunks):
    b = pl.program_id(0)
    c = pl.program_id(1)
    n_b = pl.num_programs(0)
    step = b * n_chunks + c
    slot = step % 2
    args = (pt_ref, caches, bufs, sems, layer)

    @pl.when(step == 0)
    def _():
        for cp in _chunk_copies(*args, b, c, slot, n_pg, n_chunks, True):
            cp.start()

    for cp in _chunk_copies(*args, b, c, slot, n_pg, n_chunks, False):
        cp.wait()

    wrap = c == n_chunks - 1
    c_next = jnp.where(wrap, 0, c + 1)
    b_next = jnp.where(wrap, jnp.where(b == n_b - 1, 0, b + 1), b)

    def prefetch():
        for cp in _chunk_copies(*args, b_next, c_next, 1 - slot, n_pg, n_chunks, True):
            cp.start()

    def finish():
        @pl.when(step == n_b * n_chunks - 1)
        def _():
            for cp in _chunk_copies(*args, b_next, c_next, 1 - slot, n_pg, n_chunks, False):
                cp.wait()

    return slot, prefetch, finish


def _fox_dec_kernel(pt_ref, q_ref, kn_ref, vn_ref, cc_ref, ck_hbm, cv_hbm, cf_hbm, o_ref,
                    m_ref, l_ref, acc_ref, tot_ref, kbuf, vbuf, fbuf, sems, *, layer, n_pg, n_chunks, n_new):
    slot, prefetch, finish = _page_pipeline(pt_ref, (ck_hbm, cv_hbm, cf_hbm), (kbuf, vbuf, fbuf), sems,
                                            layer, n_pg, n_chunks)
    k_pages = [kbuf.at[slot, i] for i in range(n_pg)]
    v_pages = [vbuf.at[slot, i] for i in range(n_pg)]
    f_pages = [fbuf.at[slot, i] for i in range(n_pg)]
    c = pl.program_id(1)
    hd = N_HEADS * DH
    n_rows = n_new * N_HEADS
    lane = lax.broadcasted_iota(jnp.int32, (N_HEADS, hd), 1)
    sub = lax.broadcasted_iota(jnp.int32, (N_HEADS, hd), 0)
    hmask = (lane // DH) == sub
    q_new = q_ref[...]
    q_bd = jnp.concatenate(
        [jnp.where(hmask, jnp.broadcast_to(q_new[t:t + 1, :], (N_HEADS, hd)), 0.0) for t in range(n_new)],
        axis=0)
    ccol = cc_ref[...]
    trow = lax.broadcasted_iota(jnp.int32, (n_rows, 1), 0) // N_HEADS

    @pl.when(c == 0)
    def _():
        tot_ref[...] = jnp.zeros_like(tot_ref)
        m = l = acc = None
        for s_idx in range(n_new):
            k_row = kn_ref[s_idx:s_idx + 1, :]
            v_row = vn_ref[s_idx:s_idx + 1, :]
            c_key = jnp.concatenate([ccol[s_idx * N_HEADS:(s_idx + 1) * N_HEADS, :]] * n_new, axis=0)
            z = jnp.sum(q_bd * k_row, axis=1, keepdims=True) - c_key
            if s_idx == 0:
                m, l, acc = z, jnp.ones_like(z), jnp.broadcast_to(v_row, (n_rows, hd))
            else:
                z = jnp.where(trow >= s_idx, z, NEG_INF)
                m, l, acc = _softmax_step(m, l, acc, z, lambda p, v_row=v_row: p * v_row, ccol)
        m_ref[...] = m
        l_ref[...] = l
        acc_ref[...] = acc

    prefetch()
    lf = jnp.concatenate([f_pages[i][...] for i in range(n_pg)], axis=0)
    lane_pg = lax.broadcasted_iota(jnp.int32, lf.shape, 1)
    inc = lf
    d = 1
    while d < LANES:
        inc = inc + jnp.where(lane_pg < LANES - d, pltpu.roll(inc, LANES - d, axis=1), 0.0)
        d *= 2
    exc = inc - lf
    run = tot_ref[...]
    sufs = [None] * n_pg
    for i in reversed(range(n_pg)):
        rows = slice(i * N_HEADS, (i + 1) * N_HEADS)
        sufs[i] = exc[rows, :] + run
        run = run + inc[rows, 0:1]
    tot_ref[...] = run
    suffix = jnp.concatenate(sufs, axis=1)

    qb = q_bd.astype(BF16)
    z = jnp.concatenate([_mm(qb, k_pages[i][...].astype(BF16)) for i in range(n_pg)], axis=1)
    z = z + jnp.concatenate([suffix] * n_new, axis=0)

    def pv(p):
        out = None
        for i in range(n_pg):
            part = lax.dot_general(p[:, i * LANES:(i + 1) * LANES].astype(BF16), v_pages[i][...].astype(BF16),
                                   _NT, preferred_element_type=F32)
            out = part if out is None else out + part
        return out

    m, l, acc = _softmax_step(m_ref[...], l_ref[...], acc_ref[...], z, pv, ccol)
    m_ref[...] = m
    l_ref[...] = l
    acc_ref[...] = acc

    @pl.when(c == n_chunks - 1)
    def _():
        o = acc / l
        for t in range(n_new):
            o_ref[t:t + 1, :] = jnp.sum(jnp.where(hmask, o[t * N_HEADS:(t + 1) * N_HEADS, :], 0.0),
                                        axis=0, keepdims=True)

    finish()


def _small_spec(shape):
    return pl.BlockSpec((None,) + shape, lambda b, c, pt: (b,) + (0,) * len(shape))


def _fox_dec(page_table, q, k_new, v_new, ccol, cache_k, cache_v, cache_f, layer, n_pg):
    n_b, n_new, hd = q.shape
    n_pages = page_table.shape[1]
    n_chunks = n_pages // n_pg
    page = cache_k.shape[3]
    n_rows = n_new * N_HEADS
    hbm = pl.BlockSpec(memory_space=pl.ANY)
    in_specs = [_small_spec((n_new, hd)), _small_spec((n_new, hd)), _small_spec((n_new, hd)),
                _small_spec((n_rows, 1)), hbm, hbm, hbm]
    kern = functools.partial(_fox_dec_kernel, layer=layer, n_pg=n_pg, n_chunks=n_chunks, n_new=n_new)
    return pl.pallas_call(
        kern,
        grid_spec=pltpu.PrefetchScalarGridSpec(
            num_scalar_prefetch=1, grid=(n_b, n_chunks), in_specs=in_specs,
            out_specs=pl.BlockSpec((None, n_new, hd), lambda b, c, pt: (b, 0, 0)),
            scratch_shapes=[pltpu.VMEM((n_rows, 1), F32), pltpu.VMEM((n_rows, 1), F32),
                            pltpu.VMEM((n_rows, hd), F32), pltpu.VMEM((N_HEADS, 1), F32),
                            pltpu.VMEM((2, n_pg, hd, page), F32), pltpu.VMEM((2, n_pg, hd, page), F32),
                            pltpu.VMEM((2, n_pg, N_HEADS, page), F32), pltpu.SemaphoreType.DMA((2, 3))]),
        out_shape=jax.ShapeDtypeStruct((n_b, n_new, hd), F32),
        compiler_params=_params(("arbitrary", "arbitrary")),
        name="fox_decode",
    )(page_table, q, k_new, v_new, ccol, cache_k, cache_v, cache_f)


def _mla_dec_kernel(pt_ref, qa_ref, qr_ref, ln_ref, rn_ref, cl_hbm, cr_hbm, o_ref,
                    m_ref, l_ref, acc_ref, lbuf, rbuf, sems, *, layer, n_pg, n_chunks, n_new):
    slot, prefetch, finish = _page_pipeline(pt_ref, (cl_hbm, cr_hbm), (lbuf, rbuf), sems, layer, n_pg, n_chunks)
    l_pages = [lbuf.at[slot, i] for i in range(n_pg)]
    r_pages = [rbuf.at[slot, i] for i in range(n_pg)]
    c = pl.program_id(1)
    n_rows = n_new * N_HEADS
    qa = qa_ref[...]
    qr = qr_ref[...]
    trow = lax.broadcasted_iota(jnp.int32, (n_rows, 1), 0) // N_HEADS

    @pl.when(c == 0)
    def _():
        m = l = acc = None
        for s_idx in range(n_new):
            lat_row = ln_ref[s_idx:s_idx + 1, :]
            kr_row = rn_ref[s_idx:s_idx + 1, :]
            sc = (jnp.sum(qa * lat_row, axis=1, keepdims=True)
                  + jnp.sum(qr * kr_row, axis=1, keepdims=True))
            if s_idx == 0:
                m, l, acc = sc, jnp.ones_like(sc), jnp.broadcast_to(lat_row, (n_rows, KV_LORA))
            else:
                sc = jnp.where(trow >= s_idx, sc, NEG_INF)
                m, l, acc = _softmax_step(m, l, acc, sc, lambda p, lat_row=lat_row: p * lat_row)
        m_ref[...] = m
        l_ref[...] = l
        acc_ref[...] = acc

    prefetch()
    qab = qa.astype(BF16)
    qrb = qr.astype(BF16)
    lat = [l_pages[i][...].astype(BF16) for i in range(n_pg)]
    z = jnp.concatenate([lax.dot_general(qab, lat[i], _NT, preferred_element_type=F32)
                         + _mm(qrb, r_pages[i][...].astype(BF16)) for i in range(n_pg)], axis=1)

    def pv(p):
        out = None
        for i in range(n_pg):
            part = _mm(p[:, i * LANES:(i + 1) * LANES].astype(BF16), lat[i])
            out = part if out is None else out + part
        return out

    m, l, acc = _softmax_step(m_ref[...], l_ref[...], acc_ref[...], z, pv)
    m_ref[...] = m
    l_ref[...] = l
    acc_ref[...] = acc

    @pl.when(c == n_chunks - 1)
    def _():
        o_ref[...] = acc / l

    finish()


def _mla_dec(page_table, qabs, qrope, lat_new, kr_new, cache_lat, cache_kr, layer, n_pg):
    n_b, n_rows, _ = qabs.shape
    n_new = lat_new.shape[1]
    n_pages = page_table.shape[1]
    n_chunks = n_pages // n_pg
    page = cache_lat.shape[2]
    hbm = pl.BlockSpec(memory_space=pl.ANY)
    in_specs = [_small_spec((n_rows, KV_LORA)), _small_spec((n_rows, ROPE)), _small_spec((n_new, KV_LORA)),
                _small_spec((n_new, ROPE)), hbm, hbm]
    kern = functools.partial(_mla_dec_kernel, layer=layer, n_pg=n_pg, n_chunks=n_chunks, n_new=n_new)
    return pl.pallas_call(
        kern,
        grid_spec=pltpu.PrefetchScalarGridSpec(
            num_scalar_prefetch=1, grid=(n_b, n_chunks), in_specs=in_specs,
            out_specs=pl.BlockSpec((None, n_rows, KV_LORA), lambda b, c, pt: (b, 0, 0)),
            scratch_shapes=[pltpu.VMEM((n_rows, 1), F32), pltpu.VMEM((n_rows, 1), F32),
                            pltpu.VMEM((n_rows, KV_LORA), F32),
                            pltpu.VMEM((2, n_pg, page, KV_LORA), F32), pltpu.VMEM((2, n_pg, ROPE, page), F32),
                            pltpu.SemaphoreType.DMA((2, 2))]),
        out_shape=jax.ShapeDtypeStruct((n_b, n_rows, KV_LORA), F32),
        compiler_params=_params(("arbitrary", "arbitrary")),
        name="mla_decode",
    )(page_table, qabs, qrope, lat_new, kr_new, cache_lat, cache_kr)


def _merge_kernel(*refs, latent_out, alpha):
    (x_ref, shift_ref, scale_ref, gate_ref, a_ref, of_ref, om_ref, wg_ref, wc_ref, wf_ref) = refs[:10]
    refs = refs[10:]
    if latent_out:
        bduv_ref = refs[0]
        refs = refs[1:]
    wm_ref, wo_ref, g_ref, b_ref, o_ref = refs
    x = x_ref[...]
    h = (_ln(x) * (1.0 + scale_ref[...]) + shift_ref[...]).astype(BF16)
    y_conv = _mm(a_ref[...].astype(BF16), wc_ref[...])
    y_fox = _mm(of_ref[...].astype(BF16), wf_ref[...])
    om = om_ref[...].astype(BF16)
    if latent_out:
        om = _mm(om, bduv_ref[...]).astype(BF16)
    y_mla = _mm(om, wm_ref[...])
    merged = (_sigmoid(_mm(h, wg_ref[:, 0:D_MODEL])) * y_conv
              + _sigmoid(_mm(h, wg_ref[:, D_MODEL:2 * D_MODEL])) * y_fox
              + _sigmoid(_mm(h, wg_ref[:, 2 * D_MODEL:3 * D_MODEL])) * y_mla)
    mix = _mm(merged.astype(BF16), wo_ref[...])
    o_ref[...] = _ln(alpha * x + gate_ref[...] * mix) * g_ref[...] + b_ref[...]


def _merge(x, mod_arr, mod_specs, a, o_fox, o_mla, w, *, latent_out, tm, alpha):
    rows = x.shape[0]
    row_spec = lambda width: pl.BlockSpec((tm, width), lambda i: (i, 0))
    weights = [w["w_gate"], w["conv_w_out"], w["fox_w_o"]] + ([w["bd_uv"]] if latent_out else []) + [
        w["mla_w_o"], w["w_out"], w["ln1_g"], w["ln1_b"]]
    in_specs = ([row_spec(D_MODEL)] + list(mod_specs) + [row_spec(a.shape[1]), row_spec(o_fox.shape[1]),
                                                         row_spec(o_mla.shape[1])]
                + [_const_spec(t.shape) for t in weights])
    kern = functools.partial(_merge_kernel, latent_out=latent_out, alpha=alpha)
    return pl.pallas_call(
        kern, grid=(rows // tm,), in_specs=in_specs, out_specs=row_spec(D_MODEL),
        out_shape=jax.ShapeDtypeStruct((rows, D_MODEL), F32),
        compiler_params=_params(("parallel",)),
        name="merge",
    )(x, mod_arr, mod_arr, mod_arr, a, o_fox, o_mla, *weights)


def _ffn_kernel(x_ref, shift_ref, scale_ref, gate_ref, wu_ref, wd_ref, g_ref, b_ref, o_ref, *, alpha, chunk):
    x = x_ref[...]
    h = (_ln(x) * (1.0 + scale_ref[...]) + shift_ref[...]).astype(BF16)
    ff = jnp.zeros(x.shape, F32)
    for s in range(0, D_FF, chunk):
        t = jnp.maximum(_mm(h, wu_ref[:, s:s + chunk]), 0.0)
        ff = ff + _mm((t * t).astype(BF16), wd_ref[s:s + chunk, :])
    o_ref[...] = _ln(alpha * x + gate_ref[...] * ff) * g_ref[...] + b_ref[...]


def _ffn(x, mod_arr, mod_specs, w, *, tm, alpha):
    rows = x.shape[0]
    row_spec = pl.BlockSpec((tm, D_MODEL), lambda i: (i, 0))
    weights = [w["w_up"], w["w_down"], w["ln2_g"], w["ln2_b"]]
    kern = functools.partial(_ffn_kernel, alpha=alpha, chunk=1024)
    return pl.pallas_call(
        kern, grid=(rows // tm,), in_specs=[row_spec] + list(mod_specs) + [_const_spec(t.shape) for t in weights],
        out_specs=row_spec, out_shape=jax.ShapeDtypeStruct((rows, D_MODEL), F32),
        compiler_params=_params(("parallel",)),
        name="ffn",
    )(x, mod_arr, mod_arr, mod_arr, *weights)


def _swap_halves(w):
    half = w.shape[-1] // 2
    return jnp.concatenate([w[..., half:], w[..., :half]], axis=-1)


def _block_diag(blocks):
    n, r, c = blocks.shape
    eye = jnp.eye(n, dtype=blocks.dtype)
    return (eye[:, None, :, None] * blocks[:, :, None, :]).reshape(n * r, n * c)


def _layer_weights(p, n_new):
    w_in = p["w_in"]
    hd = N_HEADS * DH
    o_f = 2 * C_CONV + 3 * hd
    o_ql = o_f + N_HEADS
    o_kv = o_ql + Q_LORA
    o_kr = o_kv + KV_LORA
    o_g = o_kr + ROPE
    w_f = w_in[:, o_f:o_ql]
    w_kr = w_in[:, o_kr:o_g]
    w_krr = _swap_halves(w_kr)
    zeros = lambda n: jnp.zeros((D_MODEL, n), F32)
    w = {}
    w["w_main"] = w_in[:, :o_f].astype(BF16)
    w["w_ql"] = w_in[:, o_ql:o_kv].astype(BF16)
    w["w_kv"] = w_in[:, o_kv:o_kr].astype(BF16)
    w["w_s"] = jnp.concatenate([w_kr, w_krr, w_f, zeros(LANES - 2 * ROPE - N_HEADS)], axis=1).astype(BF16)
    w["b_f"] = jnp.zeros((1, LANES), F32).at[0, 2 * ROPE:2 * ROPE + N_HEADS].set(p["b_f"])
    w["g_q"] = p["mla_q_norm_g"].reshape(1, Q_LORA)
    w["g_kv"] = p["mla_kv_norm_g"].reshape(1, KV_LORA)
    w["w_gate"] = w_in[:, o_g:].astype(BF16)
    w["w_kr"] = jnp.concatenate([zeros(DH), w_kr, zeros(HEAD_PAD - DH - ROPE)], axis=1).astype(BF16)
    w["w_krr"] = jnp.concatenate([zeros(DH), w_krr, zeros(HEAD_PAD - DH - ROPE)], axis=1).astype(BF16)
    w_qb = p["mla_w_qb"].reshape(Q_LORA, N_HEADS, DH + ROPE)
    qz = lambda n: jnp.zeros((Q_LORA, N_HEADS, n), F32)
    w["w_qa"] = jnp.concatenate([w_qb, qz(HEAD_PAD - DH - ROPE)], axis=2).reshape(Q_LORA, -1).astype(BF16)
    w["w_qb"] = jnp.concatenate([qz(DH), _swap_halves(w_qb[..., DH:]), qz(HEAD_PAD - DH - ROPE)],
                                axis=2).reshape(Q_LORA, -1).astype(BF16)
    w_uk = p["mla_w_uk"]
    w["w_ka"] = jnp.concatenate([w_uk, jnp.zeros((KV_LORA, N_HEADS, HEAD_PAD - DH), F32)],
                                axis=2).reshape(KV_LORA, -1).astype(BF16)
    w["w_uv"] = p["mla_w_uv"].reshape(KV_LORA, hd).astype(BF16)
    w["w_qn"] = w_qb[..., :DH].reshape(Q_LORA, hd).astype(BF16)
    w["w_qr"] = w_qb[..., DH:].reshape(Q_LORA, N_HEADS * ROPE).astype(BF16)
    w["w_qrr"] = _swap_halves(w_qb[..., DH:]).reshape(Q_LORA, N_HEADS * ROPE).astype(BF16)
    w["bd_uk"] = _block_diag(jnp.transpose(w_uk, (1, 2, 0))).astype(BF16)
    w["bd_uv"] = _block_diag(jnp.transpose(p["mla_w_uv"], (1, 0, 2))).astype(BF16)
    cw = p["conv_w"]
    w["conv_w"] = jnp.concatenate([cw, jnp.zeros((_HALO - CONV_WIDTH, C_CONV), F32)], axis=0)
    n_st = CONV_WIDTH - 1
    idx = jnp.arange(n_st + n_new)[None, :] - jnp.arange(n_new)[:, None]
    taps = jnp.where(((idx >= 0) & (idx < CONV_WIDTH))[..., None], cw[jnp.clip(idx, 0, CONV_WIDTH - 1)], 0.0)
    w["conv_ws"] = taps[:, :n_st]
    w["conv_wu"] = taps[:, n_st:]
    for name in ("conv_b", "conv_ln_g", "conv_ln_b"):
        w[name] = p[name].reshape(1, C_CONV)
    for name in ("conv_w_out", "fox_w_o", "mla_w_o", "w_out", "w_up", "w_down"):
        w[name] = p[name].astype(BF16)
    for name in ("ln1_g", "ln1_b", "ln2_g", "ln2_b"):
        w[name] = p[name].reshape(1, D_MODEL)
    return w


def _rope_tables(pos, reps):
    half = ROPE // 2
    inv_freq = ROPE_BASE ** (-jnp.arange(half, dtype=F32) / half)
    ang = pos.astype(F32)[:, None] * inv_freq
    cos, sin = jnp.cos(ang), jnp.sin(ang)
    cos32 = jnp.concatenate([cos, cos], axis=1)
    sin32 = jnp.concatenate([-sin, sin], axis=1)
    n = pos.shape[0]
    tabs = {"cos32": cos32, "sin32": sin32,
            "cos128": jnp.concatenate([jnp.ones((n, DH), F32), cos32, jnp.zeros((n, HEAD_PAD - DH - ROPE), F32)], 1),
            "sin128": jnp.concatenate([jnp.zeros((n, DH), F32), sin32, jnp.zeros((n, HEAD_PAD - DH - ROPE), F32)], 1),
            "cos256": jnp.tile(cos32, (1, N_HEADS)), "sin256": jnp.tile(sin32, (1, N_HEADS))}
    if reps > 1:
        tabs = {k: jnp.tile(v, (reps, 1)) for k, v in tabs.items()}
    return tabs


def _mod_specs_seq(cols, tiles_per_seq):
    return [pl.BlockSpec((None, 1, D_MODEL), functools.partial(
        lambda i, c: (i // tiles_per_seq, 0, c), c=c)) for c in cols]


def _mod_specs_rows(cols, tm):
    return [pl.BlockSpec((tm, D_MODEL), functools.partial(lambda i, c: (i, c), c=c)) for c in cols]


def kernel(x_prompt, x_sample, cache_fox_k, cache_fox_v, cache_fox_logf, cache_mla_latent, cache_mla_krope,
           state_conv, page_table, c_prompt, c_sample, w_ada, b_ada, w_in, b_f, conv_w, conv_b, conv_ln_g,
           conv_ln_b, conv_w_out, fox_w_o, mla_q_norm_g, mla_w_qb, mla_kv_norm_g, mla_w_uk, mla_w_uv, mla_w_o,
           w_out, ln1_g, ln1_b, w_up, w_down, ln2_g, ln2_b):
    n_layers = w_in.shape[0]
    n_b, seq_len, _ = x_prompt.shape
    n_s, n_new, _ = x_sample.shape
    page = cache_fox_k.shape[2]
    n_pages = page_table.shape[1]
    n_past = n_pages * page
    alpha = float((2 * n_layers) ** 0.25)
    hd = N_HEADS * DH
    n_pairs = N_HEADS // 2

    tm_p = min(512, seq_len)
    rows_s = n_s * n_new
    tm_s = min(256, rows_s)
    blk = min(512, seq_len)
    n_pg_fox = min(16, n_pages)
    n_pg_mla = min(32, n_pages)
    nb_conv = min(8, n_s)

    n_pool = cache_fox_k.shape[1]
    ck = jnp.transpose(cache_fox_k, (0, 1, 3, 4, 2)).reshape(n_layers, n_pool, hd, page)
    cv = jnp.transpose(cache_fox_v, (0, 1, 3, 4, 2)).reshape(n_layers, n_pool, hd, page)
    cf = jnp.transpose(cache_fox_logf, (0, 1, 3, 2))
    ckr = jnp.transpose(cache_mla_krope, (0, 1, 3, 2))

    m_all = n_b + n_s
    m_pad = -(-m_all // 8) * 8
    c_all = jnp.concatenate([c_prompt, c_sample, jnp.zeros((m_pad - m_all, D_MODEL), F32)], axis=0)
    mod_all = _ada(c_all, w_ada, b_ada)

    tabs_p = _rope_tables(jnp.arange(seq_len, dtype=jnp.int32), 1)
    tabs_s = _rope_tables(n_past + jnp.arange(n_new, dtype=jnp.int32), n_s)

    params = dict(w_in=w_in, b_f=b_f, conv_w=conv_w, conv_b=conv_b, conv_ln_g=conv_ln_g, conv_ln_b=conv_ln_b,
                  conv_w_out=conv_w_out, fox_w_o=fox_w_o, mla_q_norm_g=mla_q_norm_g, mla_w_qb=mla_w_qb,
                  mla_kv_norm_g=mla_kv_norm_g, mla_w_uk=mla_w_uk, mla_w_uv=mla_w_uv, mla_w_o=mla_w_o,
                  w_out=w_out, ln1_g=ln1_g, ln1_b=ln1_b, w_up=w_up, w_down=w_down, ln2_g=ln2_g, ln2_b=ln2_b)

    xp = x_prompt.reshape(n_b * seq_len, D_MODEL)
    xs = x_sample.reshape(rows_s, D_MODEL)
    rows_p, rows_smp = [], []
    tps = seq_len // tm_p
    for l in range(n_layers):
        w = _layer_weights({k: v[l] for k, v in params.items()}, n_new)
        mod_p = mod_all[l, :n_b].reshape(n_b, 1, 6 * D_MODEL)
        mod_s = jnp.repeat(mod_all[l, n_b:n_b + n_s], n_new, axis=0)

        (u, kf, vf, logf, cs, lat, kr, qf, kb, vb, mq, mk, mv) = _inproj(
            xp, mod_p, _mod_specs_seq((0, 1), tps), w, tabs_p, prompt=True, seg=seq_len, tm=tm_p)
        a_p, st_p = _conv_prompt(u, w, n_b, seq_len, tm_p)
        c4 = cs.reshape(n_b, seq_len, n_pairs, 2)
        cq = jnp.transpose(c4, (0, 2, 1, 3))
        ckk = jnp.transpose(c4.reshape(n_b, seq_len // blk, blk, n_pairs, 2), (0, 3, 1, 4, 2))
        o_fox = _attn_prompt(qf, kb, vb, (cq, ckk), n_b, seq_len, LANES, blk)
        o_mla = _attn_prompt(mq, mk, mv, None, n_b, seq_len, 2 * HEAD_PAD, blk)
        x1 = _merge(xp, mod_p, _mod_specs_seq((0, 1, 2), tps), a_p, o_fox, o_mla, w,
                    latent_out=False, tm=tm_p, alpha=alpha)
        xp = _ffn(x1, mod_p, _mod_specs_seq((3, 4, 5), tps), w, tm=tm_p, alpha=alpha)
        rows_p.append((kf.reshape(n_b, seq_len, N_HEADS, DH), vf.reshape(n_b, seq_len, N_HEADS, DH),
                       logf.reshape(n_b, seq_len, N_HEADS), lat.reshape(n_b, seq_len, KV_LORA),
                       kr.reshape(n_b, seq_len, ROPE), st_p))

        (u, kf, vf, logf, cs, lat, kr, qf, qabs, qrope) = _inproj(
            xs, mod_s, _mod_specs_rows((0, 1), tm_s), w, tabs_s, prompt=False, seg=n_new, tm=tm_s)
        u_t = jnp.transpose(u.reshape(n_s, n_new, C_CONV), (1, 0, 2))
        st_t = jnp.transpose(state_conv[l], (1, 0, 2))
        a_t, nst_t = _conv_sample(u_t, st_t, w, nb_conv)
        a_s = jnp.transpose(a_t, (1, 0, 2)).reshape(rows_s, C_CONV)
        st_s = jnp.transpose(nst_t, (1, 0, 2))
        o_fox = _fox_dec(page_table, qf.reshape(n_s, n_new, hd), kf.reshape(n_s, n_new, hd),
                         vf.reshape(n_s, n_new, hd), cs.reshape(n_s, n_new * N_HEADS, 1),
                         ck, cv, cf, l, n_pg_fox)
        o_lat = _mla_dec(page_table, qabs.reshape(n_s, n_new * N_HEADS, KV_LORA),
                         qrope.reshape(n_s, n_new * N_HEADS, ROPE), lat.reshape(n_s, n_new, KV_LORA),
                         kr.reshape(n_s, n_new, ROPE), cache_mla_latent, ckr, l, n_pg_mla)
        x1 = _merge(xs, mod_s, _mod_specs_rows((0, 1, 2), tm_s), a_s, o_fox.reshape(rows_s, hd),
                    o_lat.reshape(rows_s, N_HEADS * KV_LORA), w, latent_out=True, tm=tm_s, alpha=alpha)
        xs = _ffn(x1, mod_s, _mod_specs_rows((3, 4, 5), tm_s), w, tm=tm_s, alpha=alpha)
        rows_smp.append((kf.reshape(n_s, n_new, N_HEADS, DH), vf.reshape(n_s, n_new, N_HEADS, DH),
                         logf.reshape(n_s, n_new, N_HEADS), lat.reshape(n_s, n_new, KV_LORA),
                         kr.reshape(n_s, n_new, ROPE), st_s))

    def stk(rows, i):
        return jnp.stack([r[i] for r in rows])

    return ((xp.reshape(n_b, seq_len, D_MODEL), xs.reshape(n_s, n_new, D_MODEL))
            + tuple(stk(rows_p, i) for i in range(6)) + tuple(stk(rows_smp, i) for i in range(6)))
```

```python
import functools
import math

import jax
import jax.numpy as jnp
from jax import lax
from jax.experimental import pallas as pl
from jax.experimental.pallas import tpu as pltpu

F32 = jnp.float32
BF16 = jnp.bfloat16

D_MODEL = 1024
C_CONV = D_MODEL // 2
CONV_WIDTH = 31
N_HEADS = 8
DH = D_MODEL // 16
ROPE = D_MODEL // 32
KV_LORA = D_MODEL // 4
Q_LORA = 3 * KV_LORA
D_FF = 4 * D_MODEL
ROPE_BASE = 10000.0
NEG_INF = -1e30
LOG2E = math.log2(math.e)
LANES = 128
HEAD_PAD = 128
VMEM_LIMIT = 56 * 1024 * 1024

_NT = (((1,), (1,)), ((), ()))


def _params(sem, vmem=VMEM_LIMIT):
    return pltpu.CompilerParams(dimension_semantics=sem, vmem_limit_bytes=vmem)


def _const_spec(shape):
    nd = len(shape)
    return pl.BlockSpec(shape, lambda *_: (0,) * nd, pipeline_mode=pl.Buffered(1))


def _ln(x, eps=1e-5):
    mu = jnp.mean(x, axis=-1, keepdims=True)
    xc = x - mu
    var = jnp.mean(xc * xc, axis=-1, keepdims=True)
    return xc * lax.rsqrt(var + eps)


def _rms(x, eps=1e-6):
    return x * lax.rsqrt(jnp.mean(x * x, axis=-1, keepdims=True) + eps)


def _sigmoid(x):
    return 1.0 / (1.0 + jnp.exp(-x))


def _log_sigmoid(x):
    return jnp.minimum(x, 0.0) - jnp.log(1.0 + jnp.exp(-jnp.abs(x)))


def _mm(a, b):
    return jnp.dot(a, b, preferred_element_type=F32)


def _ada_kernel(c_ref, w_ref, b_ref, o_ref):
    c = c_ref[...]
    a = (c * _sigmoid(c)).astype(BF16)
    o_ref[...] = _mm(a, w_ref[...].astype(BF16)) + b_ref[...]


def _ada(c_all, w_ada, b_ada):
    n_layers, _, n_out = w_ada.shape
    m = c_all.shape[0]
    tn = 1536
    return pl.pallas_call(
        _ada_kernel,
        grid=(n_layers, n_out // tn),
        in_specs=[pl.BlockSpec((m, D_MODEL), lambda l, j: (0, 0)),
                  pl.BlockSpec((None, D_MODEL, tn), lambda l, j: (l, 0, j)),
                  pl.BlockSpec((None, 1, tn), lambda l, j: (l, 0, j))],
        out_specs=pl.BlockSpec((None, m, tn), lambda l, j: (l, 0, j)),
        out_shape=jax.ShapeDtypeStruct((n_layers, m, n_out), F32),
        compiler_params=_params(("parallel", "parallel")),
        name="ada_mod",
    )(c_all, w_ada, b_ada.reshape(n_layers, 1, n_out))


def _seg_cumsum(x, seg):
    rows = x.shape[0]
    row = lax.broadcasted_iota(jnp.int32, x.shape, 0)
    rin = row % seg if seg < rows else row
    d = 1
    while d < min(seg, rows):
        x = x + jnp.where(rin >= d, pltpu.roll(x, d, axis=0), 0.0)
        d *= 2
    return x


def _inproj_kernel(*refs, prompt, seg, tiles_per_seq, fox_scale, mla_scale):
    (x_ref, shift_ref, scale_ref, wmain_ref, wql_ref, wkv_ref, ws_ref, bf_ref, gq_ref, gkv_ref,
     cos32_ref, sin32_ref) = refs[:12]
    refs = refs[12:]
    if prompt:
        (wkr_ref, wkrr_ref, cos128_ref, sin128_ref, wqa_ref, wqb_ref, wka_ref, wuv_ref) = refs[:8]
        refs = refs[8:]
        (u_ref, kf_ref, vf_ref, logf_ref, c_ref, lat_ref, kr_ref,
         qf_ref, kb_ref, vb_ref, mq_ref, mk_ref, mv_ref, carry_ref) = refs
    else:
        (wqn_ref, bduk_ref, wqr_ref, wqrr_ref, cos256_ref, sin256_ref) = refs[:6]
        refs = refs[6:]
        (u_ref, kf_ref, vf_ref, logf_ref, c_ref, lat_ref, kr_ref,
         qf_ref, qabs_ref, qrope_ref, carry_ref) = refs

    i = pl.program_id(0)
    x = x_ref[...]
    tm = x.shape[0]
    h = (_ln(x) * (1.0 + scale_ref[...]) + shift_ref[...]).astype(BF16)

    u_ref[...] = _mm(h, wmain_ref[:, 0:C_CONV]) * _sigmoid(_mm(h, wmain_ref[:, C_CONV:2 * C_CONV]))
    o = 2 * C_CONV
    hd = N_HEADS * DH
    qf_ref[...] = (_mm(h, wmain_ref[:, o:o + hd]) * fox_scale).astype(qf_ref.dtype)
    kf = _mm(h, wmain_ref[:, o + hd:o + 2 * hd])
    vf = _mm(h, wmain_ref[:, o + 2 * hd:o + 3 * hd])
    kf_ref[...] = kf
    vf_ref[...] = vf
    if prompt:
        kb_ref[...] = kf.astype(BF16)
        vb_ref[...] = vf.astype(BF16)

    zs = _mm(h, ws_ref[...])
    kr_ref[...] = zs[:, 0:ROPE] * cos32_ref[...] + zs[:, ROPE:2 * ROPE] * sin32_ref[...]
    logf = _log_sigmoid(zs + bf_ref[...])
    logf_ref[...] = logf[:, 2 * ROPE:2 * ROPE + N_HEADS]
    cs = _seg_cumsum(logf, seg)
    if seg >= tm:
        @pl.when(i % tiles_per_seq == 0)
        def _():
            carry_ref[...] = jnp.zeros_like(carry_ref)
        cs = cs + carry_ref[...]
        carry_ref[...] = cs[tm - 1:tm, :]
    c_ref[...] = cs[:, 2 * ROPE:2 * ROPE + N_HEADS]

    latn = _rms(_mm(h, wkv_ref[...])) * gkv_ref[...]
    lat_ref[...] = latn
    qn = (_rms(_mm(h, wql_ref[...])) * gq_ref[...]).astype(BF16)
    if prompt:
        latb = latn.astype(BF16)
        cosq = cos128_ref[...]
        sinq = sin128_ref[...]
        krt = _mm(h, wkr_ref[...]) * cosq + _mm(h, wkrr_ref[...]) * sinq
        per_head = lambda t: jnp.concatenate([t] * N_HEADS, axis=1)
        mq_ref[...] = ((_mm(qn, wqa_ref[...]) * per_head(cosq) + _mm(qn, wqb_ref[...]) * per_head(sinq))
                       * mla_scale).astype(BF16)
        mk_ref[...] = (_mm(latb, wka_ref[...]) + per_head(krt)).astype(BF16)
        mv_ref[...] = _mm(latb, wuv_ref[...]).astype(BF16)
    else:
        qnope = _mm(qn, wqn_ref[...]).astype(BF16)
        qabs_ref[...] = _mm(qnope, bduk_ref[...]) * mla_scale
        qrope_ref[...] = (_mm(qn, wqr_ref[...]) * cos256_ref[...]
                          + _mm(qn, wqrr_ref[...]) * sin256_ref[...]) * mla_scale


def _inproj(x, shift_spec_arr, mod_specs, w, tabs, *, prompt, seg, tm):
    rows = x.shape[0]
    n_tiles = rows // tm
    n_tab = tabs["cos32"].shape[0] // tm
    tiles_per_seq = max(seg // tm, 1)
    exp_scale = LOG2E if prompt else 1.0

    def row_spec(width):
        return pl.BlockSpec((tm, width), lambda i: (i, 0))

    def tab_spec(width):
        return pl.BlockSpec((tm, width), lambda i: (i % n_tab, 0))

    common_w = [w["w_main"], w["w_ql"], w["w_kv"], w["w_s"], w["b_f"], w["g_q"], w["g_kv"]]
    in_arrays = [x, shift_spec_arr, shift_spec_arr] + common_w + [tabs["cos32"], tabs["sin32"]]
    in_specs = ([row_spec(D_MODEL), mod_specs[0], mod_specs[1]]
                + [_const_spec(a.shape) for a in common_w] + [tab_spec(ROPE), tab_spec(ROPE)])
    hd = N_HEADS * DH
    out_shapes = [jax.ShapeDtypeStruct((rows, C_CONV), F32),
                  jax.ShapeDtypeStruct((rows, hd), F32),
                  jax.ShapeDtypeStruct((rows, hd), F32),
                  jax.ShapeDtypeStruct((rows, N_HEADS), F32),
                  jax.ShapeDtypeStruct((rows, N_HEADS), F32),
                  jax.ShapeDtypeStruct((rows, KV_LORA), F32),
                  jax.ShapeDtypeStruct((rows, ROPE), F32)]
    out_specs = [row_spec(C_CONV), row_spec(hd), row_spec(hd), row_spec(N_HEADS), row_spec(N_HEADS),
                 row_spec(KV_LORA), row_spec(ROPE)]
    if prompt:
        extra_w = [w["w_kr"], w["w_krr"]]
        in_arrays += extra_w + [tabs["cos128"], tabs["sin128"]]
        in_specs += [_const_spec(a.shape) for a in extra_w] + [tab_spec(LANES), tab_spec(LANES)]
        extra_w2 = [w["w_qa"], w["w_qb"], w["w_ka"], w["w_uv"]]
        in_arrays += extra_w2
        in_specs += [_const_spec(a.shape) for a in extra_w2]
        wide = N_HEADS * HEAD_PAD
        out_shapes += [jax.ShapeDtypeStruct((rows, hd), BF16)] * 3 + [
            jax.ShapeDtypeStruct((rows, wide), BF16), jax.ShapeDtypeStruct((rows, wide), BF16),
            jax.ShapeDtypeStruct((rows, hd), BF16)]
        out_specs += [row_spec(hd)] * 3 + [row_spec(wide), row_spec(wide), row_spec(hd)]
    else:
        extra_w = [w["w_qn"], w["bd_uk"], w["w_qr"], w["w_qrr"]]
        in_arrays += extra_w + [tabs["cos256"], tabs["sin256"]]
        in_specs += [_const_spec(a.shape) for a in extra_w] + [tab_spec(N_HEADS * ROPE)] * 2
        out_shapes += [jax.ShapeDtypeStruct((rows, hd), F32),
                       jax.ShapeDtypeStruct((rows, N_HEADS * KV_LORA), F32),
                       jax.ShapeDtypeStruct((rows, N_HEADS * ROPE), F32)]
        out_specs += [row_spec(hd), row_spec(N_HEADS * KV_LORA), row_spec(N_HEADS * ROPE)]

    kern = functools.partial(_inproj_kernel, prompt=prompt, seg=seg, tiles_per_seq=tiles_per_seq,
                             fox_scale=float(DH ** -0.5) * exp_scale,
                             mla_scale=float((DH + ROPE) ** -0.5) * exp_scale)
    return pl.pallas_call(
        kern, grid=(n_tiles,), in_specs=in_specs, out_specs=out_specs, out_shape=out_shapes,
        scratch_shapes=[pltpu.VMEM((1, LANES), F32)],
        compiler_params=_params(("arbitrary",)),
        name="inproj_prompt" if prompt else "inproj_sample",
    )(*in_arrays)


_HALO = 32


_SUBLANES = 8


def _conv_prompt_kernel(u_ref, w_ref, b_ref, g_ref, beta_ref, a_ref, st_ref, ext_ref, sh_ref):
    j = pl.program_id(1)
    tm = u_ref.shape[0]

    @pl.when(j == 0)
    def _():
        ext_ref[0:_HALO, :] = jnp.zeros((_HALO, C_CONV), F32)

    ext_ref[_HALO:_HALO + tm, :] = u_ref[...]
    off = _HALO - (CONV_WIDTH - 1)
    n_sh = sh_ref.shape[1]
    for r in range(1, _SUBLANES):
        sh_ref[r - 1] = ext_ref[r:r + n_sh, :]
    y = jnp.zeros((tm, C_CONV), F32) + b_ref[...]
    for k in range(CONV_WIDTH):
        q, r = divmod(off + k, _SUBLANES)
        win = ext_ref[q * _SUBLANES:q * _SUBLANES + tm, :] if r == 0 else sh_ref[r - 1, q * _SUBLANES:q * _SUBLANES + tm, :]
        y = y + win * w_ref[k:k + 1, :]
    yn = _ln(y) * g_ref[...] + beta_ref[...]
    a_ref[...] = (yn * _sigmoid(yn)).astype(a_ref.dtype)
    st_ref[...] = ext_ref[tm + off:tm + _HALO, :]
    ext_ref[0:_HALO, :] = ext_ref[tm:tm + _HALO, :]


def _conv_prompt(u, w, n_seq, seq_len, tm):
    nt = seq_len // tm
    return pl.pallas_call(
        _conv_prompt_kernel,
        grid=(n_seq, nt),
        in_specs=[pl.BlockSpec((tm, C_CONV), lambda b, j: (b * nt + j, 0)),
                  _const_spec(w["conv_w"].shape), _const_spec((1, C_CONV)), _const_spec((1, C_CONV)),
                  _const_spec((1, C_CONV))],
        out_specs=[pl.BlockSpec((tm, C_CONV), lambda b, j: (b * nt + j, 0)),
                   pl.BlockSpec((None, CONV_WIDTH - 1, C_CONV), lambda b, j: (b, 0, 0))],
        out_shape=[jax.ShapeDtypeStruct((n_seq * seq_len, C_CONV), BF16),
                   jax.ShapeDtypeStruct((n_seq, CONV_WIDTH - 1, C_CONV), F32)],
        scratch_shapes=[pltpu.VMEM((_HALO + tm, C_CONV), F32),
                        pltpu.VMEM((_SUBLANES - 1, _HALO - _SUBLANES + tm, C_CONV), F32)],
        compiler_params=_params(("parallel", "arbitrary")),
        name="conv_prompt",
    )(u, w["conv_w"], w["conv_b"], w["conv_ln_g"], w["conv_ln_b"])


def _conv_sample_kernel(u_ref, st_ref, ws_ref, wu_ref, b_ref, g_ref, beta_ref, a_ref, nst_ref):
    n_new = u_ref.shape[0]
    n_st = st_ref.shape[0]
    for t in range(n_new):
        y = jnp.zeros(u_ref.shape[1:], F32) + b_ref[...]
        for r in range(t, n_st):
            y = y + st_ref[r] * ws_ref[t, r:r + 1, :]
        for r in range(t + 1):
            y = y + u_ref[r] * wu_ref[t, r:r + 1, :]
        yn = _ln(y) * g_ref[...] + beta_ref[...]
        a_ref[t] = yn * _sigmoid(yn)
    for r in range(n_st - n_new):
        nst_ref[r] = st_ref[r + n_new]
    for t in range(n_new):
        nst_ref[n_st - n_new + t] = u_ref[t]


def _conv_sample(u_t, st_t, w, nb):
    n_new, n_b, _ = u_t.shape
    n_st = st_t.shape[0]
    return pl.pallas_call(
        _conv_sample_kernel,
        grid=(n_b // nb,),
        in_specs=[pl.BlockSpec((n_new, nb, C_CONV), lambda i: (0, i, 0)),
                  pl.BlockSpec((n_st, nb, C_CONV), lambda i: (0, i, 0)),
                  _const_spec(w["conv_ws"].shape), _const_spec(w["conv_wu"].shape),
                  _const_spec((1, C_CONV)), _const_spec((1, C_CONV)), _const_spec((1, C_CONV))],
        out_specs=[pl.BlockSpec((n_new, nb, C_CONV), lambda i: (0, i, 0)),
                   pl.BlockSpec((n_st, nb, C_CONV), lambda i: (0, i, 0))],
        out_shape=[jax.ShapeDtypeStruct((n_new, n_b, C_CONV), F32),
                   jax.ShapeDtypeStruct((n_st, n_b, C_CONV), F32)],
        compiler_params=_params(("parallel",)),
        name="conv_sample",
    )(u_t, st_t, w["conv_ws"], w["conv_wu"], w["conv_b"], w["conv_ln_g"], w["conv_ln_b"])


def _attn_prompt_kernel(*refs, width, blk, has_bias, n_sub, n_q):
    if has_bias:
        q_ref, k_ref, v_ref, cq_ref, ck_ref, o_ref = refs
    else:
        q_ref, k_ref, v_ref, o_ref = refs
    i = pl.program_id(2)
    lane = lax.broadcasted_iota(jnp.int32, (1, LANES), 1)
    sub = blk // n_sub
    row = lax.broadcasted_iota(jnp.int32, (sub, blk), 0)
    col = lax.broadcasted_iota(jnp.int32, (sub, blk), 1)
    qs, lanes = [], []
    for e in range(2):
        if width == LANES:
            keep = (lane < DH) if e == 0 else (lane >= DH)
            qs.append(jnp.where(keep, q_ref[...], jnp.zeros((), q_ref.dtype)))
            lanes.append(slice(0, LANES))
        else:
            lanes.append(slice(e * LANES, (e + 1) * LANES))
            qs.append(q_ref[:, lanes[e]])

    def block(j, carry, masked):
        start = j * blk
        v = v_ref[pl.ds(start, blk), :]
        out = []
        for e in range(2):
            k = k_ref[pl.ds(start, blk), lanes[e]]
            for sb in range(n_sub):
                rows = slice(sb * sub, (sb + 1) * sub)
                m, l, acc = carry[e * n_sub + sb]
                z = lax.dot_general(qs[e][rows, :], k, _NT, preferred_element_type=F32)
                if has_bias:
                    z = z - ck_ref[j, e:e + 1, :] * LOG2E
                if masked:
                    z = jnp.where(col <= row + sb * sub, z, NEG_INF)
                m_new = jnp.maximum(m, jnp.max(z, axis=1, keepdims=True))
                shift = m_new
                if has_bias:
                    cq = cq_ref[rows, e:e + 1] * LOG2E
                    shift = (m_new + cq) - cq
                alpha = jnp.exp2(m - m_new)
                p = jnp.exp2(z - shift)
                l = alpha * l + jnp.sum(p, axis=1, keepdims=True)
                acc = alpha * acc + _mm(p.astype(BF16), v)
                out.append((m_new, l, acc))
        return tuple(out)

    init = (jnp.full((sub, 1), NEG_INF, F32), jnp.zeros((sub, 1), F32), jnp.zeros((sub, LANES), F32))
    for iv in range(n_q):
        @pl.when(i == iv)
        def _(iv=iv):
            carry = (init,) * (2 * n_sub)
            for j in range(iv):
                carry = block(j, carry, False)
            res = block(iv, carry, True)
            for sb in range(n_sub):
                (_, l0, acc0), (_, l1, acc1) = res[sb], res[n_sub + sb]
                o_ref[sb * sub:(sb + 1) * sub, :] = jnp.where(lane < DH, acc0 / l0, acc1 / l1).astype(o_ref.dtype)


def _attn_prompt(q, k, v, bias, n_seq, seq_len, width, blk):
    n_pairs = N_HEADS // 2
    nq = seq_len // blk
    in_specs = [pl.BlockSpec((blk, width), lambda b, p, i: (b * nq + i, p)),
                pl.BlockSpec((seq_len, width), lambda b, p, i: (b, p)),
                pl.BlockSpec((seq_len, LANES), lambda b, p, i: (b, p))]
    args = [q, k, v]
    if bias is not None:
        in_specs += [pl.BlockSpec((None, None, blk, 2), lambda b, p, i: (b, p, i, 0)),
                     pl.BlockSpec((None, None, nq, 2, blk), lambda b, p, i: (b, p, 0, 0, 0))]
        args += list(bias)
    kern = functools.partial(_attn_prompt_kernel, width=width, blk=blk, has_bias=bias is not None, n_sub=1,
                             n_q=nq)
    return pl.pallas_call(
        kern, grid=(n_seq, n_pairs, nq), in_specs=in_specs,
        out_specs=pl.BlockSpec((blk, LANES), lambda b, p, i: (b * nq + i, p)),
        out_shape=jax.ShapeDtypeStruct((n_seq * seq_len, n_pairs * LANES), BF16),
        compiler_params=_params(("parallel", "parallel", "arbitrary")),
        name="attn_prompt_fox" if bias is not None else "attn_prompt_mla",
    )(*args)


def _softmax_step(m, l, acc, z, pv, crow=None):
    m_new = jnp.maximum(m, jnp.max(z, axis=1, keepdims=True))
    shift = m_new if crow is None else (m_new + crow) - crow
    alpha = jnp.exp(m - m_new)
    p = jnp.exp(z - shift)
    return m_new, alpha * l + jnp.sum(p, axis=1, keepdims=True), alpha * acc + pv(p)


def _chunk_copies(pt_ref, caches, bufs, sems, layer, b, c, slot, n_pg, n_chunks, read_table):
    first = (n_chunks - 1 - c) * n_pg
    out = []
    for i in range(n_pg):
        page = pt_ref[b, first + i] if read_table else 0
        for a, (cache, buf) in enumerate(zip(caches, bufs)):
            out.append(pltpu.make_async_copy(cache.at[layer, page], buf.at[slot, i], sems.at[slot, a]))
    return out


def _page_pipeline(pt_ref, caches, bufs, sems, layer, n_pg, n_chunks):
    b = pl.program_id(0)
    c = pl.program_id(1)
    n_b = pl.num_programs(0)
    step = b * n_chunks + c
    slot = step % 2
    args = (pt_ref, caches, bufs, sems, layer)

    @pl.when(step == 0)
    def _():
        for cp in _chunk_copies(*args, b, c, slot, n_pg, n_chunks, True):
            cp.start()

    wrap = c == n_chunks - 1
    c_next = jnp.where(wrap, 0, c + 1)
    b_next = jnp.where(wrap, jnp.where(b == n_b - 1, 0, b + 1), b)
    for cp in _chunk_copies(*args, b_next, c_next, 1 - slot, n_pg, n_chunks, True):
        cp.start()

    for cp in _chunk_copies(*args, b, c, slot, n_pg, n_chunks, False):
        cp.wait()

    def finish():
        @pl.when(step == n_b * n_chunks - 1)
        def _():
            for cp in _chunk_copies(*args, b_next, c_next, 1 - slot, n_pg, n_chunks, False):
                cp.wait()

    return slot, finish


def _fox_dec_kernel(pt_ref, q_ref, kn_ref, vn_ref, cc_ref, ck_hbm, cv_hbm, cf_hbm, o_ref,
                    m_ref, l_ref, acc_ref, tot_ref, kbuf, vbuf, fbuf, sems, *, layer, n_pg, n_chunks, n_new):
    slot, finish = _page_pipeline(pt_ref, (ck_hbm, cv_hbm, cf_hbm), (kbuf, vbuf, fbuf), sems,
                                  layer, n_pg, n_chunks)
    k_pages = [kbuf.at[slot, i] for i in range(n_pg)]
    v_pages = [vbuf.at[slot, i] for i in range(n_pg)]
    f_pages = [fbuf.at[slot, i] for i in range(n_pg)]
    c = pl.program_id(1)
    hd = N_HEADS * DH
    n_rows = n_new * N_HEADS
    lane = lax.broadcasted_iota(jnp.int32, (N_HEADS, hd), 1)
    sub = lax.broadcasted_iota(jnp.int32, (N_HEADS, hd), 0)
    hmask = (lane // DH) == sub
    q_new = q_ref[...]
    q_bd = jnp.concatenate(
        [jnp.where(hmask, jnp.broadcast_to(q_new[t:t + 1, :], (N_HEADS, hd)), 0.0) for t in range(n_new)],
        axis=0)
    ccol = cc_ref[...]
    trow = lax.broadcasted_iota(jnp.int32, (n_rows, 1), 0) // N_HEADS

    @pl.when(c == 0)
    def _():
        tot_ref[...] = jnp.zeros_like(tot_ref)
        m = l = acc = None
        for s_idx in range(n_new):
            k_row = kn_ref[s_idx:s_idx + 1, :]
            v_row = vn_ref[s_idx:s_idx + 1, :]
            c_key = jnp.concatenate([ccol[s_idx * N_HEADS:(s_idx + 1) * N_HEADS, :]] * n_new, axis=0)
            z = jnp.sum(q_bd * k_row, axis=1, keepdims=True) - c_key
            if s_idx == 0:
                m, l, acc = z, jnp.ones_like(z), jnp.broadcast_to(v_row, (n_rows, hd))
            else:
                z = jnp.where(trow >= s_idx, z, NEG_INF)
                m, l, acc = _softmax_step(m, l, acc, z, lambda p, v_row=v_row: p * v_row, ccol)
        m_ref[...] = m
        l_ref[...] = l
        acc_ref[...] = acc

    lf = jnp.concatenate([f_pages[i][...] for i in range(n_pg)], axis=0)
    lane_pg = lax.broadcasted_iota(jnp.int32, lf.shape, 1)
    inc = lf
    d = 1
    while d < LANES:
        inc = inc + jnp.where(lane_pg < LANES - d, pltpu.roll(inc, LANES - d, axis=1), 0.0)
        d *= 2
    exc = inc - lf
    run = tot_ref[...]
    sufs = [None] * n_pg
    for i in reversed(range(n_pg)):
        rows = slice(i * N_HEADS, (i + 1) * N_HEADS)
        sufs[i] = exc[rows, :] + run
        run = run + inc[rows, 0:1]
    tot_ref[...] = run
    suffix = jnp.concatenate(sufs, axis=1)

    qb = q_bd.astype(BF16)
    z = jnp.concatenate([_mm(qb, k_pages[i][...].astype(BF16)) for i in range(n_pg)], axis=1)
    z = z + jnp.concatenate([suffix] * n_new, axis=0)

    def pv(p):
        out = None
        for i in range(n_pg):
            part = lax.dot_general(p[:, i * LANES:(i + 1) * LANES].astype(BF16), v_pages[i][...].astype(BF16),
                                   _NT, preferred_element_type=F32)
            out = part if out is None else out + part
        return out

    m, l, acc = _softmax_step(m_ref[...], l_ref[...], acc_ref[...], z, pv, ccol)
    m_ref[...] = m
    l_ref[...] = l
    acc_ref[...] = acc

    @pl.when(c == n_chunks - 1)
    def _():
        o = acc / l
        for t in range(n_new):
            o_ref[t:t + 1, :] = jnp.sum(jnp.where(hmask, o[t * N_HEADS:(t + 1) * N_HEADS, :], 0.0),
                                        axis=0, keepdims=True)

    finish()


def _small_spec(shape):
    return pl.BlockSpec((None,) + shape, lambda b, c, pt: (b,) + (0,) * len(shape))


def _fox_dec(page_table, q, k_new, v_new, ccol, cache_k, cache_v, cache_f, layer, n_pg):
    n_b, n_new, hd = q.shape
    n_pages = page_table.shape[1]
    n_chunks = n_pages // n_pg
    page = cache_k.shape[3]
    n_rows = n_new * N_HEADS
    hbm = pl.BlockSpec(memory_space=pl.ANY)
    in_specs = [_small_spec((n_new, hd)), _small_spec((n_new, hd)), _small_spec((n_new, hd)),
                _small_spec((n_rows, 1)), hbm, hbm, hbm]
    kern = functools.partial(_fox_dec_kernel, layer=layer, n_pg=n_pg, n_chunks=n_chunks, n_new=n_new)
    return pl.pallas_call(
        kern,
        grid_spec=pltpu.PrefetchScalarGridSpec(
            num_scalar_prefetch=1, grid=(n_b, n_chunks), in_specs=in_specs,
            out_specs=pl.BlockSpec((None, n_new, hd), lambda b, c, pt: (b, 0, 0)),
            scratch_shapes=[pltpu.VMEM((n_rows, 1), F32), pltpu.VMEM((n_rows, 1), F32),
                            pltpu.VMEM((n_rows, hd), F32), pltpu.VMEM((N_HEADS, 1), F32),
                            pltpu.VMEM((2, n_pg, hd, page), F32), pltpu.VMEM((2, n_pg, hd, page), F32),
                            pltpu.VMEM((2, n_pg, N_HEADS, page), F32), pltpu.SemaphoreType.DMA((2, 3))]),
        out_shape=jax.ShapeDtypeStruct((n_b, n_new, hd), F32),
        compiler_params=_params(("arbitrary", "arbitrary")),
        name="fox_decode",
    )(page_table, q, k_new, v_new, ccol, cache_k, cache_v, cache_f)


def _mla_dec_kernel(pt_ref, qa_ref, qr_ref, ln_ref, rn_ref, cl_hbm, cr_hbm, o_ref,
                    m_ref, l_ref, acc_ref, lbuf, rbuf, sems, *, layer, n_pg, n_chunks, n_new):
    slot, finish = _page_pipeline(pt_ref, (cl_hbm, cr_hbm), (lbuf, rbuf), sems, layer, n_pg, n_chunks)
    l_pages = [lbuf.at[slot, i] for i in range(n_pg)]
    r_pages = [rbuf.at[slot, i] for i in range(n_pg)]
    c = pl.program_id(1)
    n_rows = n_new * N_HEADS
    qa = qa_ref[...]
    qr = qr_ref[...]
    trow = lax.broadcasted_iota(jnp.int32, (n_rows, 1), 0) // N_HEADS

    @pl.when(c == 0)
    def _():
        m = l = acc = None
        for s_idx in range(n_new):
            lat_row = ln_ref[s_idx:s_idx + 1, :]
            kr_row = rn_ref[s_idx:s_idx + 1, :]
            sc = (jnp.sum(qa * lat_row, axis=1, keepdims=True)
                  + jnp.sum(qr * kr_row, axis=1, keepdims=True))
            if s_idx == 0:
                m, l, acc = sc, jnp.ones_like(sc), jnp.broadcast_to(lat_row, (n_rows, KV_LORA))
            else:
                sc = jnp.where(trow >= s_idx, sc, NEG_INF)
                m, l, acc = _softmax_step(m, l, acc, sc, lambda p, lat_row=lat_row: p * lat_row)
        m_ref[...] = m
        l_ref[...] = l
        acc_ref[...] = acc

    qab = qa.astype(BF16)
    qrb = qr.astype(BF16)
    lat = [l_pages[i][...].astype(BF16) for i in range(n_pg)]
    z = jnp.concatenate([lax.dot_general(qab, lat[i], _NT, preferred_element_type=F32)
                         + _mm(qrb, r_pages[i][...].astype(BF16)) for i in range(n_pg)], axis=1)

    def pv(p):
        out = None
        for i in range(n_pg):
            part = _mm(p[:, i * LANES:(i + 1) * LANES].astype(BF16), lat[i])
            out = part if out is None else out + part
        return out

    m, l, acc = _softmax_step(m_ref[...], l_ref[...], acc_ref[...], z, pv)
    m_ref[...] = m
    l_ref[...] = l
    acc_ref[...] = acc

    @pl.when(c == n_chunks - 1)
    def _():
        o_ref[...] = acc / l

    finish()


def _mla_dec(page_table, qabs, qrope, lat_new, kr_new, cache_lat, cache_kr, layer, n_pg):
    n_b, n_rows, _ = qabs.shape
    n_new = lat_new.shape[1]
    n_pages = page_table.shape[1]
    n_chunks = n_pages // n_pg
    page = cache_lat.shape[2]
    hbm = pl.BlockSpec(memory_space=pl.ANY)
    in_specs = [_small_spec((n_rows, KV_LORA)), _small_spec((n_rows, ROPE)), _small_spec((n_new, KV_LORA)),
                _small_spec((n_new, ROPE)), hbm, hbm]
    kern = functools.partial(_mla_dec_kernel, layer=layer, n_pg=n_pg, n_chunks=n_chunks, n_new=n_new)
    return pl.pallas_call(
        kern,
        grid_spec=pltpu.PrefetchScalarGridSpec(
            num_scalar_prefetch=1, grid=(n_b, n_chunks), in_specs=in_specs,
            out_specs=pl.BlockSpec((None, n_rows, KV_LORA), lambda b, c, pt: (b, 0, 0)),
            scratch_shapes=[pltpu.VMEM((n_rows, 1), F32), pltpu.VMEM((n_rows, 1), F32),
                            pltpu.VMEM((n_rows, KV_LORA), F32),
                            pltpu.VMEM((2, n_pg, page, KV_LORA), F32), pltpu.VMEM((2, n_pg, ROPE, page), F32),
                            pltpu.SemaphoreType.DMA((2, 2))]),
        out_shape=jax.ShapeDtypeStruct((n_b, n_rows, KV_LORA), F32),
        compiler_params=_params(("arbitrary", "arbitrary")),
        name="mla_decode",
    )(page_table, qabs, qrope, lat_new, kr_new, cache_lat, cache_kr)


def _merge_kernel(*refs, latent_out, alpha):
    (x_ref, shift_ref, scale_ref, gate_ref, a_ref, of_ref, om_ref, wg_ref, wc_ref, wf_ref) = refs[:10]
    refs = refs[10:]
    if latent_out:
        bduv_ref = refs[0]
        refs = refs[1:]
    wm_ref, wo_ref, g_ref, b_ref, o_ref = refs
    x = x_ref[...]
    h = (_ln(x) * (1.0 + scale_ref[...]) + shift_ref[...]).astype(BF16)
    y_conv = _mm(a_ref[...].astype(BF16), wc_ref[...])
    y_fox = _mm(of_ref[...].astype(BF16), wf_ref[...])
    om = om_ref[...].astype(BF16)
    if latent_out:
        om = _mm(om, bduv_ref[...]).astype(BF16)
    y_mla = _mm(om, wm_ref[...])
    merged = (_sigmoid(_mm(h, wg_ref[:, 0:D_MODEL])) * y_conv
              + _sigmoid(_mm(h, wg_ref[:, D_MODEL:2 * D_MODEL])) * y_fox
              + _sigmoid(_mm(h, wg_ref[:, 2 * D_MODEL:3 * D_MODEL])) * y_mla)
    mix = _mm(merged.astype(BF16), wo_ref[...])
    o_ref[...] = _ln(alpha * x + gate_ref[...] * mix) * g_ref[...] + b_ref[...]


def _merge(x, mod_arr, mod_specs, a, o_fox, o_mla, w, *, latent_out, tm, alpha):
    rows = x.shape[0]
    row_spec = lambda width: pl.BlockSpec((tm, width), lambda i: (i, 0))
    weights = [w["w_gate"], w["conv_w_out"], w["fox_w_o"]] + ([w["bd_uv"]] if latent_out else []) + [
        w["mla_w_o"], w["w_out"], w["ln1_g"], w["ln1_b"]]
    in_specs = ([row_spec(D_MODEL)] + list(mod_specs) + [row_spec(a.shape[1]), row_spec(o_fox.shape[1]),
                                                         row_spec(o_mla.shape[1])]
                + [_const_spec(t.shape) for t in weights])
    kern = functools.partial(_merge_kernel, latent_out=latent_out, alpha=alpha)
    return pl.pallas_call(
        kern, grid=(rows // tm,), in_specs=in_specs, out_specs=row_spec(D_MODEL),
        out_shape=jax.ShapeDtypeStruct((rows, D_MODEL), F32),
        compiler_params=_params(("parallel",)),
        name="merge",
    )(x, mod_arr, mod_arr, mod_arr, a, o_fox, o_mla, *weights)


def _ffn_kernel(x_ref, shift_ref, scale_ref, gate_ref, wu_ref, wd_ref, g_ref, b_ref, o_ref, *, alpha, chunk):
    x = x_ref[...]
    h = (_ln(x) * (1.0 + scale_ref[...]) + shift_ref[...]).astype(BF16)
    ff = jnp.zeros(x.shape, F32)
    for s in range(0, D_FF, chunk):
        t = jnp.maximum(_mm(h, wu_ref[:, s:s + chunk]), 0.0)
        ff = ff + _mm((t * t).astype(BF16), wd_ref[s:s + chunk, :])
    o_ref[...] = _ln(alpha * x + gate_ref[...] * ff) * g_ref[...] + b_ref[...]


def _ffn(x, mod_arr, mod_specs, w, *, tm, alpha):
    rows = x.shape[0]
    row_spec = pl.BlockSpec((tm, D_MODEL), lambda i: (i, 0))
    weights = [w["w_up"], w["w_down"], w["ln2_g"], w["ln2_b"]]
    kern = functools.partial(_ffn_kernel, alpha=alpha, chunk=1024)
    return pl.pallas_call(
        kern, grid=(rows // tm,), in_specs=[row_spec] + list(mod_specs) + [_const_spec(t.shape) for t in weights],
        out_specs=row_spec, out_shape=jax.ShapeDtypeStruct((rows, D_MODEL), F32),
        compiler_params=_params(("parallel",)),
        name="ffn",
    )(x, mod_arr, mod_arr, mod_arr, *weights)


def _swap_halves(w):
    half = w.shape[-1] // 2
    return jnp.concatenate([w[..., half:], w[..., :half]], axis=-1)


def _block_diag(blocks):
    n, r, c = blocks.shape
    eye = jnp.eye(n, dtype=blocks.dtype)
    return (eye[:, None, :, None] * blocks[:, :, None, :]).reshape(n * r, n * c)


def _layer_weights(p, n_new):
    w_in = p["w_in"]
    hd = N_HEADS * DH
    o_f = 2 * C_CONV + 3 * hd
    o_ql = o_f + N_HEADS
    o_kv = o_ql + Q_LORA
    o_kr = o_kv + KV_LORA
    o_g = o_kr + ROPE
    w_f = w_in[:, o_f:o_ql]
    w_kr = w_in[:, o_kr:o_g]
    w_krr = _swap_halves(w_kr)
    zeros = lambda n: jnp.zeros((D_MODEL, n), F32)
    w = {}
    w["w_main"] = w_in[:, :o_f].astype(BF16)
    w["w_ql"] = w_in[:, o_ql:o_kv].astype(BF16)
    w["w_kv"] = w_in[:, o_kv:o_kr].astype(BF16)
    w["w_s"] = jnp.concatenate([w_kr, w_krr, w_f, zeros(LANES - 2 * ROPE - N_HEADS)], axis=1).astype(BF16)
    w["b_f"] = jnp.zeros((1, LANES), F32).at[0, 2 * ROPE:2 * ROPE + N_HEADS].set(p["b_f"])
    w["g_q"] = p["mla_q_norm_g"].reshape(1, Q_LORA)
    w["g_kv"] = p["mla_kv_norm_g"].reshape(1, KV_LORA)
    w["w_gate"] = w_in[:, o_g:].astype(BF16)
    w["w_kr"] = jnp.concatenate([zeros(DH), w_kr, zeros(HEAD_PAD - DH - ROPE)], axis=1).astype(BF16)
    w["w_krr"] = jnp.concatenate([zeros(DH), w_krr, zeros(HEAD_PAD - DH - ROPE)], axis=1).astype(BF16)
    w_qb = p["mla_w_qb"].reshape(Q_LORA, N_HEADS, DH + ROPE)
    qz = lambda n: jnp.zeros((Q_LORA, N_HEADS, n), F32)
    w["w_qa"] = jnp.concatenate([w_qb, qz(HEAD_PAD - DH - ROPE)], axis=2).reshape(Q_LORA, -1).astype(BF16)
    w["w_qb"] = jnp.concatenate([qz(DH), _swap_halves(w_qb[..., DH:]), qz(HEAD_PAD - DH - ROPE)],
                                axis=2).reshape(Q_LORA, -1).astype(BF16)
    w_uk = p["mla_w_uk"]
    w["w_ka"] = jnp.concatenate([w_uk, jnp.zeros((KV_LORA, N_HEADS, HEAD_PAD - DH), F32)],
                                axis=2).reshape(KV_LORA, -1).astype(BF16)
    w["w_uv"] = p["mla_w_uv"].reshape(KV_LORA, hd).astype(BF16)
    w["w_qn"] = w_qb[..., :DH].reshape(Q_LORA, hd).astype(BF16)
    w["w_qr"] = w_qb[..., DH:].reshape(Q_LORA, N_HEADS * ROPE).astype(BF16)
    w["w_qrr"] = _swap_halves(w_qb[..., DH:]).reshape(Q_LORA, N_HEADS * ROPE).astype(BF16)
    w["bd_uk"] = _block_diag(jnp.transpose(w_uk, (1, 2, 0))).astype(BF16)
    w["bd_uv"] = _block_diag(jnp.transpose(p["mla_w_uv"], (1, 0, 2))).astype(BF16)
    cw = p["conv_w"]
    w["conv_w"] = jnp.concatenate([cw, jnp.zeros((_HALO - CONV_WIDTH, C_CONV), F32)], axis=0)
    n_st = CONV_WIDTH - 1
    idx = jnp.arange(n_st + n_new)[None, :] - jnp.arange(n_new)[:, None]
    taps = jnp.where(((idx >= 0) & (idx < CONV_WIDTH))[..., None], cw[jnp.clip(idx, 0, CONV_WIDTH - 1)], 0.0)
    w["conv_ws"] = taps[:, :n_st]
    w["conv_wu"] = taps[:, n_st:]
    for name in ("conv_b", "conv_ln_g", "conv_ln_b"):
        w[name] = p[name].reshape(1, C_CONV)
    for name in ("conv_w_out", "fox_w_o", "mla_w_o", "w_out", "w_up", "w_down"):
        w[name] = p[name].astype(BF16)
    for name in ("ln1_g", "ln1_b", "ln2_g", "ln2_b"):
        w[name] = p[name].reshape(1, D_MODEL)
    return w


def _rope_tables(pos, reps):
    half = ROPE // 2
    inv_freq = ROPE_BASE ** (-jnp.arange(half, dtype=F32) / half)
    ang = pos.astype(F32)[:, None] * inv_freq
    cos, sin = jnp.cos(ang), jnp.sin(ang)
    cos32 = jnp.concatenate([cos, cos], axis=1)
    sin32 = jnp.concatenate([-sin, sin], axis=1)
    n = pos.shape[0]
    tabs = {"cos32": cos32, "sin32": sin32,
            "cos128": jnp.concatenate([jnp.ones((n, DH), F32), cos32, jnp.zeros((n, HEAD_PAD - DH - ROPE), F32)], 1),
            "sin128": jnp.concatenate([jnp.zeros((n, DH), F32), sin32, jnp.zeros((n, HEAD_PAD - DH - ROPE), F32)], 1),
            "cos256": jnp.tile(cos32, (1, N_HEADS)), "sin256": jnp.tile(sin32, (1, N_HEADS))}
    if reps > 1:
        tabs = {k: jnp.tile(v, (reps, 1)) for k, v in tabs.items()}
    return tabs


def _mod_specs_seq(cols, tiles_per_seq):
    return [pl.BlockSpec((None, 1, D_MODEL), functools.partial(
        lambda i, c: (i // tiles_per_seq, 0, c), c=c)) for c in cols]


def _mod_specs_rows(cols, tm):
    return [pl.BlockSpec((tm, D_MODEL), functools.partial(lambda i, c: (i, c), c=c)) for c in cols]


def kernel(x_prompt, x_sample, cache_fox_k, cache_fox_v, cache_fox_logf, cache_mla_latent, cache_mla_krope,
           state_conv, page_table, c_prompt, c_sample, w_ada, b_ada, w_in, b_f, conv_w, conv_b, conv_ln_g,
           conv_ln_b, conv_w_out, fox_w_o, mla_q_norm_g, mla_w_qb, mla_kv_norm_g, mla_w_uk, mla_w_uv, mla_w_o,
           w_out, ln1_g, ln1_b, w_up, w_down, ln2_g, ln2_b):
    n_layers = w_in.shape[0]
    n_b, seq_len, _ = x_prompt.shape
    n_s, n_new, _ = x_sample.shape
    page = cache_fox_k.shape[2]
    n_pages = page_table.shape[1]
    n_past = n_pages * page
    alpha = float((2 * n_layers) ** 0.25)
    hd = N_HEADS * DH
    n_pairs = N_HEADS // 2

    tm_p = min(512, seq_len)
    rows_s = n_s * n_new
    tm_s = min(256, rows_s)
    blk = min(512, seq_len)
    n_pg_fox = min(16, n_pages)
    n_pg_mla = min(64, n_pages)
    nb_conv = min(8, n_s)

    n_pool = cache_fox_k.shape[1]
    ck = jnp.transpose(cache_fox_k, (0, 1, 3, 4, 2)).reshape(n_layers, n_pool, hd, page)
    cv = jnp.transpose(cache_fox_v, (0, 1, 3, 4, 2)).reshape(n_layers, n_pool, hd, page)
    cf = jnp.transpose(cache_fox_logf, (0, 1, 3, 2))
    ckr = jnp.transpose(cache_mla_krope, (0, 1, 3, 2))

    m_all = n_b + n_s
    m_pad = -(-m_all // 8) * 8
    c_all = jnp.concatenate([c_prompt, c_sample, jnp.zeros((m_pad - m_all, D_MODEL), F32)], axis=0)
    mod_all = _ada(c_all, w_ada, b_ada)

    tabs_p = _rope_tables(jnp.arange(seq_len, dtype=jnp.int32), 1)
    tabs_s = _rope_tables(n_past + jnp.arange(n_new, dtype=jnp.int32), n_s)

    params = dict(w_in=w_in, b_f=b_f, conv_w=conv_w, conv_b=conv_b, conv_ln_g=conv_ln_g, conv_ln_b=conv_ln_b,
                  conv_w_out=conv_w_out, fox_w_o=fox_w_o, mla_q_norm_g=mla_q_norm_g, mla_w_qb=mla_w_qb,
                  mla_kv_norm_g=mla_kv_norm_g, mla_w_uk=mla_w_uk, mla_w_uv=mla_w_uv, mla_w_o=mla_w_o,
                  w_out=w_out, ln1_g=ln1_g, ln1_b=ln1_b, w_up=w_up, w_down=w_down, ln2_g=ln2_g, ln2_b=ln2_b)

    xp = x_prompt.reshape(n_b * seq_len, D_MODEL)
    xs = x_sample.reshape(rows_s, D_MODEL)
    rows_p, rows_smp = [], []
    tps = seq_len // tm_p
    for l in range(n_layers):
        w = _layer_weights({k: v[l] for k, v in params.items()}, n_new)
        mod_p = mod_all[l, :n_b].reshape(n_b, 1, 6 * D_MODEL)
        mod_s = jnp.repeat(mod_all[l, n_b:n_b + n_s], n_new, axis=0)

        (u, kf, vf, logf, cs, lat, kr, qf, kb, vb, mq, mk, mv) = _inproj(
            xp, mod_p, _mod_specs_seq((0, 1), tps), w, tabs_p, prompt=True, seg=seq_len, tm=tm_p)
        a_p, st_p = _conv_prompt(u, w, n_b, seq_len, tm_p)
        c4 = cs.reshape(n_b, seq_len, n_pairs, 2)
        cq = jnp.transpose(c4, (0, 2, 1, 3))
        ckk = jnp.transpose(c4.reshape(n_b, seq_len // blk, blk, n_pairs, 2), (0, 3, 1, 4, 2))
        o_fox = _attn_prompt(qf, kb, vb, (cq, ckk), n_b, seq_len, LANES, blk)
        o_mla = _attn_prompt(mq, mk, mv, None, n_b, seq_len, 2 * HEAD_PAD, blk)
        x1 = _merge(xp, mod_p, _mod_specs_seq((0, 1, 2), tps), a_p, o_fox, o_mla, w,
                    latent_out=False, tm=tm_p, alpha=alpha)
        xp = _ffn(x1, mod_p, _mod_specs_seq((3, 4, 5), tps), w, tm=tm_p, alpha=alpha)
        rows_p.append((kf.reshape(n_b, seq_len, N_HEADS, DH), vf.reshape(n_b, seq_len, N_HEADS, DH),
                       logf.reshape(n_b, seq_len, N_HEADS), lat.reshape(n_b, seq_len, KV_LORA),
                       kr.reshape(n_b, seq_len, ROPE), st_p))

        (u, kf, vf, logf, cs, lat, kr, qf, qabs, qrope) = _inproj(
            xs, mod_s, _mod_specs_rows((0, 1), tm_s), w, tabs_s, prompt=False, seg=n_new, tm=tm_s)
        u_t = jnp.transpose(u.reshape(n_s, n_new, C_CONV), (1, 0, 2))
        st_t = jnp.transpose(state_conv[l], (1, 0, 2))
        a_t, nst_t = _conv_sample(u_t, st_t, w, nb_conv)
        a_s = jnp.transpose(a_t, (1, 0, 2)).reshape(rows_s, C_CONV)
        st_s = jnp.transpose(nst_t, (1, 0, 2))
        o_fox = _fox_dec(page_table, qf.reshape(n_s, n_new, hd), kf.reshape(n_s, n_new, hd),
                         vf.reshape(n_s, n_new, hd), cs.reshape(n_s, n_new * N_HEADS, 1),
                         ck, cv, cf, l, n_pg_fox)
        o_lat = _mla_dec(page_table, qabs.reshape(n_s, n_new * N_HEADS, KV_LORA),
                         qrope.reshape(n_s, n_new * N_HEADS, ROPE), lat.reshape(n_s, n_new, KV_LORA),
                         kr.reshape(n_s, n_new, ROPE), cache_mla_latent, ckr, l, n_pg_mla)
        x1 = _merge(xs, mod_s, _mod_specs_rows((0, 1, 2), tm_s), a_s, o_fox.reshape(rows_s, hd),
                    o_lat.reshape(rows_s, N_HEADS * KV_LORA), w, latent_out=True, tm=tm_s, alpha=alpha)
        xs = _ffn(x1, mod_s, _mod_specs_rows((3, 4, 5), tm_s), w, tm=tm_s, alpha=alpha)
        rows_smp.append((kf.reshape(n_s, n_new, N_HEADS, DH), vf.reshape(n_s, n_new, N_HEADS, DH),
                         logf.reshape(n_s, n_new, N_HEADS), lat.reshape(n_s, n_new, KV_LORA),
                         kr.reshape(n_s, n_new, ROPE), st_s))

    def stk(rows, i):
        return jnp.stack([r[i] for r in rows])

    return ((xp.reshape(n_b, seq_len, D_MODEL), xs.reshape(n_s, n_new, D_MODEL))
            + tuple(stk(rows_p, i) for i in range(6)) + tuple(stk(rows_smp, i) for i in range(6)))
```

```python
import functools
import math

import jax
import jax.numpy as jnp
from jax import lax
from jax.experimental import pallas as pl
from jax.experimental.pallas import tpu as pltpu

F32 = jnp.float32
BF16 = jnp.bfloat16

D_MODEL = 1024
C_CONV = D_MODEL // 2
CONV_WIDTH = 31
N_HEADS = 8
DH = D_MODEL // 16
ROPE = D_MODEL // 32
KV_LORA = D_MODEL // 4
Q_LORA = 3 * KV_LORA
D_FF = 4 * D_MODEL
ROPE_BASE = 10000.0
NEG_INF = -1e30
LOG2E = math.log2(math.e)
LANES = 128
HEAD_PAD = 128
VMEM_LIMIT = 56 * 1024 * 1024

_NT = (((1,), (1,)), ((), ()))


def _params(sem, vmem=VMEM_LIMIT):
    return pltpu.CompilerParams(dimension_semantics=sem, vmem_limit_bytes=vmem)


def _const_spec(shape):
    nd = len(shape)
    return pl.BlockSpec(shape, lambda *_: (0,) * nd, pipeline_mode=pl.Buffered(1))


def _ln(x, eps=1e-5):
    mu = jnp.mean(x, axis=-1, keepdims=True)
    xc = x - mu
    var = jnp.mean(xc * xc, axis=-1, keepdims=True)
    return xc * lax.rsqrt(var + eps)


def _rms(x, eps=1e-6):
    return x * lax.rsqrt(jnp.mean(x * x, axis=-1, keepdims=True) + eps)


def _sigmoid(x):
    return 1.0 / (1.0 + jnp.exp(-x))


def _log_sigmoid(x):
    return jnp.minimum(x, 0.0) - jnp.log(1.0 + jnp.exp(-jnp.abs(x)))


def _mm(a, b):
    return jnp.dot(a, b, preferred_element_type=F32)


def _ada_kernel(c_ref, w_ref, b_ref, o_ref):
    c = c_ref[...]
    a = (c * _sigmoid(c)).astype(BF16)
    o_ref[...] = _mm(a, w_ref[...].astype(BF16)) + b_ref[...]


def _ada(c_all, w_ada, b_ada):
    n_layers, _, n_out = w_ada.shape
    m = c_all.shape[0]
    tn = 1536
    return pl.pallas_call(
        _ada_kernel,
        grid=(n_layers, n_out // tn),
        in_specs=[pl.BlockSpec((m, D_MODEL), lambda l, j: (0, 0)),
                  pl.BlockSpec((None, D_MODEL, tn), lambda l, j: (l, 0, j)),
                  pl.BlockSpec((None, 1, tn), lambda l, j: (l, 0, j))],
        out_specs=pl.BlockSpec((None, m, tn), lambda l, j: (l, 0, j)),
        out_shape=jax.ShapeDtypeStruct((n_layers, m, n_out), F32),
        compiler_params=_params(("parallel", "parallel")),
        name="ada_mod",
    )(c_all, w_ada, b_ada.reshape(n_layers, 1, n_out))


def _seg_cumsum(x, seg):
    rows = x.shape[0]
    row = lax.broadcasted_iota(jnp.int32, x.shape, 0)
    rin = row % seg if seg < rows else row
    d = 1
    while d < min(seg, rows):
        x = x + jnp.where(rin >= d, pltpu.roll(x, d, axis=0), 0.0)
        d *= 2
    return x


def _inproj_kernel(*refs, prompt, seg, tiles_per_seq, fox_scale, mla_scale):
    (x_ref, shift_ref, scale_ref, wmain_ref, wql_ref, wkv_ref, ws_ref, bf_ref, gq_ref, gkv_ref,
     cos32_ref, sin32_ref) = refs[:12]
    refs = refs[12:]
    if prompt:
        (wkr_ref, wkrr_ref, cos128_ref, sin128_ref, wqa_ref, wqb_ref, wka_ref, wuv_ref) = refs[:8]
        refs = refs[8:]
        (u_ref, kf_ref, vf_ref, logf_ref, c_ref, lat_ref, kr_ref,
         qf_ref, kb_ref, vb_ref, mq_ref, mk_ref, mv_ref, carry_ref) = refs
    else:
        (wqn_ref, bduk_ref, wqr_ref, wqrr_ref, cos256_ref, sin256_ref) = refs[:6]
        refs = refs[6:]
        (u_ref, kf_ref, vf_ref, logf_ref, c_ref, lat_ref, kr_ref,
         qf_ref, qabs_ref, qrope_ref, carry_ref) = refs

    i = pl.program_id(0)
    x = x_ref[...]
    tm = x.shape[0]
    h = (_ln(x) * (1.0 + scale_ref[...]) + shift_ref[...]).astype(BF16)

    u_ref[...] = _mm(h, wmain_ref[:, 0:C_CONV]) * _sigmoid(_mm(h, wmain_ref[:, C_CONV:2 * C_CONV]))
    o = 2 * C_CONV
    hd = N_HEADS * DH
    qf_ref[...] = (_mm(h, wmain_ref[:, o:o + hd]) * fox_scale).astype(qf_ref.dtype)
    kf = _mm(h, wmain_ref[:, o + hd:o + 2 * hd])
    vf = _mm(h, wmain_ref[:, o + 2 * hd:o + 3 * hd])
    kf_ref[...] = kf
    vf_ref[...] = vf
    if prompt:
        kb_ref[...] = kf.astype(BF16)
        vb_ref[...] = vf.astype(BF16)

    zs = _mm(h, ws_ref[...])
    kr_ref[...] = zs[:, 0:ROPE] * cos32_ref[...] + zs[:, ROPE:2 * ROPE] * sin32_ref[...]
    logf = _log_sigmoid(zs + bf_ref[...])
    logf_ref[...] = logf[:, 2 * ROPE:2 * ROPE + N_HEADS]
    cs = _seg_cumsum(logf, seg)
    if seg >= tm:
        @pl.when(i % tiles_per_seq == 0)
        def _():
            carry_ref[...] = jnp.zeros_like(carry_ref)
        cs = cs + carry_ref[...]
        carry_ref[...] = cs[tm - 1:tm, :]
    c_ref[...] = cs[:, 2 * ROPE:2 * ROPE + N_HEADS]

    latn = _rms(_mm(h, wkv_ref[...])) * gkv_ref[...]
    lat_ref[...] = latn
    qn = (_rms(_mm(h, wql_ref[...])) * gq_ref[...]).astype(BF16)
    if prompt:
        latb = latn.astype(BF16)
        cosq = cos128_ref[...]
        sinq = sin128_ref[...]
        krt = _mm(h, wkr_ref[...]) * cosq + _mm(h, wkrr_ref[...]) * sinq
        per_head = lambda t: jnp.concatenate([t] * N_HEADS, axis=1)
        mq_ref[...] = ((_mm(qn, wqa_ref[...]) * per_head(cosq) + _mm(qn, wqb_ref[...]) * per_head(sinq))
                       * mla_scale).astype(BF16)
        mk_ref[...] = (_mm(latb, wka_ref[...]) + per_head(krt)).astype(BF16)
        mv_ref[...] = _mm(latb, wuv_ref[...]).astype(BF16)
    else:
        qnope = _mm(qn, wqn_ref[...]).astype(BF16)
        qabs_ref[...] = _mm(qnope, bduk_ref[...]) * mla_scale
        qrope_ref[...] = (_mm(qn, wqr_ref[...]) * cos256_ref[...]
                          + _mm(qn, wqrr_ref[...]) * sin256_ref[...]) * mla_scale


def _inproj(x, shift_spec_arr, mod_specs, w, tabs, *, prompt, seg, tm):
    rows = x.shape[0]
    n_tiles = rows // tm
    n_tab = tabs["cos32"].shape[0] // tm
    tiles_per_seq = max(seg // tm, 1)
    exp_scale = LOG2E if prompt else 1.0

    def row_spec(width):
        return pl.BlockSpec((tm, width), lambda i: (i, 0))

    def tab_spec(width):
        return pl.BlockSpec((tm, width), lambda i: (i % n_tab, 0))

    common_w = [w["w_main"], w["w_ql"], w["w_kv"], w["w_s"], w["b_f"], w["g_q"], w["g_kv"]]
    in_arrays = [x, shift_spec_arr, shift_spec_arr] + common_w + [tabs["cos32"], tabs["sin32"]]
    in_specs = ([row_spec(D_MODEL), mod_specs[0], mod_specs[1]]
                + [_const_spec(a.shape) for a in common_w] + [tab_spec(ROPE), tab_spec(ROPE)])
    hd = N_HEADS * DH
    out_shapes = [jax.ShapeDtypeStruct((rows, C_CONV), F32),
                  jax.ShapeDtypeStruct((rows, hd), F32),
                  jax.ShapeDtypeStruct((rows, hd), F32),
                  jax.ShapeDtypeStruct((rows, N_HEADS), F32),
                  jax.ShapeDtypeStruct((rows, N_HEADS), F32),
                  jax.ShapeDtypeStruct((rows, KV_LORA), F32),
                  jax.ShapeDtypeStruct((rows, ROPE), F32)]
    out_specs = [row_spec(C_CONV), row_spec(hd), row_spec(hd), row_spec(N_HEADS), row_spec(N_HEADS),
                 row_spec(KV_LORA), row_spec(ROPE)]
    if prompt:
        extra_w = [w["w_kr"], w["w_krr"]]
        in_arrays += extra_w + [tabs["cos128"], tabs["sin128"]]
        in_specs += [_const_spec(a.shape) for a in extra_w] + [tab_spec(LANES), tab_spec(LANES)]
        extra_w2 = [w["w_qa"], w["w_qb"], w["w_ka"], w["w_uv"]]
        in_arrays += extra_w2
        in_specs += [_const_spec(a.shape) for a in extra_w2]
        wide = N_HEADS * HEAD_PAD
        out_shapes += [jax.ShapeDtypeStruct((rows, hd), BF16)] * 3 + [
            jax.ShapeDtypeStruct((rows, wide), BF16), jax.ShapeDtypeStruct((rows, wide), BF16),
            jax.ShapeDtypeStruct((rows, hd), BF16)]
        out_specs += [row_spec(hd)] * 3 + [row_spec(wide), row_spec(wide), row_spec(hd)]
    else:
        extra_w = [w["w_qn"], w["bd_uk"], w["w_qr"], w["w_qrr"]]
        in_arrays += extra_w + [tabs["cos256"], tabs["sin256"]]
        in_specs += [_const_spec(a.shape) for a in extra_w] + [tab_spec(N_HEADS * ROPE)] * 2
        out_shapes += [jax.ShapeDtypeStruct((rows, hd), F32),
                       jax.ShapeDtypeStruct((rows, N_HEADS * KV_LORA), F32),
                       jax.ShapeDtypeStruct((rows, N_HEADS * ROPE), F32)]
        out_specs += [row_spec(hd), row_spec(N_HEADS * KV_LORA), row_spec(N_HEADS * ROPE)]

    kern = functools.partial(_inproj_kernel, prompt=prompt, seg=seg, tiles_per_seq=tiles_per_seq,
                             fox_scale=float(DH ** -0.5) * exp_scale,
                             mla_scale=float((DH + ROPE) ** -0.5) * exp_scale)
    return pl.pallas_call(
        kern, grid=(n_tiles,), in_specs=in_specs, out_specs=out_specs, out_shape=out_shapes,
        scratch_shapes=[pltpu.VMEM((1, LANES), F32)],
        compiler_params=_params(("arbitrary",)),
        name="inproj_prompt" if prompt else "inproj_sample",
    )(*in_arrays)


_HALO = 32


_SUBLANES = 8


def _conv_prompt_kernel(u_ref, w_ref, b_ref, g_ref, beta_ref, a_ref, st_ref, ext_ref, sh_ref):
    j = pl.program_id(1)
    tm = u_ref.shape[0]

    @pl.when(j == 0)
    def _():
        ext_ref[0:_HALO, :] = jnp.zeros((_HALO, C_CONV), F32)

    ext_ref[_HALO:_HALO + tm, :] = u_ref[...]
    off = _HALO - (CONV_WIDTH - 1)
    n_sh = sh_ref.shape[1]
    for r in range(1, _SUBLANES):
        sh_ref[r - 1] = ext_ref[r:r + n_sh, :]
    y = jnp.zeros((tm, C_CONV), F32) + b_ref[...]
    for k in range(CONV_WIDTH):
        q, r = divmod(off + k, _SUBLANES)
        win = ext_ref[q * _SUBLANES:q * _SUBLANES + tm, :] if r == 0 else sh_ref[r - 1, q * _SUBLANES:q * _SUBLANES + tm, :]
        y = y + win * w_ref[k:k + 1, :]
    yn = _ln(y) * g_ref[...] + beta_ref[...]
    a_ref[...] = (yn * _sigmoid(yn)).astype(a_ref.dtype)
    st_ref[...] = ext_ref[tm + off:tm + _HALO, :]
    ext_ref[0:_HALO, :] = ext_ref[tm:tm + _HALO, :]


def _conv_prompt(u, w, n_seq, seq_len, tm):
    nt = seq_len // tm
    return pl.pallas_call(
        _conv_prompt_kernel,
        grid=(n_seq, nt),
        in_specs=[pl.BlockSpec((tm, C_CONV), lambda b, j: (b * nt + j, 0)),
                  _const_spec(w["conv_w"].shape), _const_spec((1, C_CONV)), _const_spec((1, C_CONV)),
                  _const_spec((1, C_CONV))],
        out_specs=[pl.BlockSpec((tm, C_CONV), lambda b, j: (b * nt + j, 0)),
                   pl.BlockSpec((None, CONV_WIDTH - 1, C_CONV), lambda b, j: (b, 0, 0))],
        out_shape=[jax.ShapeDtypeStruct((n_seq * seq_len, C_CONV), BF16),
                   jax.ShapeDtypeStruct((n_seq, CONV_WIDTH - 1, C_CONV), F32)],
        scratch_shapes=[pltpu.VMEM((_HALO + tm, C_CONV), F32),
                        pltpu.VMEM((_SUBLANES - 1, _HALO - _SUBLANES + tm, C_CONV), F32)],
        compiler_params=_params(("parallel", "arbitrary")),
        name="conv_prompt",
    )(u, w["conv_w"], w["conv_b"], w["conv_ln_g"], w["conv_ln_b"])


def _conv_sample_kernel(u_ref, st_ref, ws_ref, wu_ref, b_ref, g_ref, beta_ref, a_ref, nst_ref):
    n_new = u_ref.shape[0]
    n_st = st_ref.shape[0]
    for t in range(n_new):
        y = jnp.zeros(u_ref.shape[1:], F32) + b_ref[...]
        for r in range(t, n_st):
            y = y + st_ref[r] * ws_ref[t, r:r + 1, :]
        for r in range(t + 1):
            y = y + u_ref[r] * wu_ref[t, r:r + 1, :]
        yn = _ln(y) * g_ref[...] + beta_ref[...]
        a_ref[t] = yn * _sigmoid(yn)
    for r in range(n_st - n_new):
        nst_ref[r] = st_ref[r + n_new]
    for t in range(n_new):
        nst_ref[n_st - n_new + t] = u_ref[t]


def _conv_sample(u_t, st_t, w, nb):
    n_new, n_b, _ = u_t.shape
    n_st = st_t.shape[0]
    return pl.pallas_call(
        _conv_sample_kernel,
        grid=(n_b // nb,),
        in_specs=[pl.BlockSpec((n_new, nb, C_CONV), lambda i: (0, i, 0)),
                  pl.BlockSpec((n_st, nb, C_CONV), lambda i: (0, i, 0)),
                  _const_spec(w["conv_ws"].shape), _const_spec(w["conv_wu"].shape),
                  _const_spec((1, C_CONV)), _const_spec((1, C_CONV)), _const_spec((1, C_CONV))],
        out_specs=[pl.BlockSpec((n_new, nb, C_CONV), lambda i: (0, i, 0)),
                   pl.BlockSpec((n_st, nb, C_CONV), lambda i: (0, i, 0))],
        out_shape=[jax.ShapeDtypeStruct((n_new, n_b, C_CONV), F32),
                   jax.ShapeDtypeStruct((n_st, n_b, C_CONV), F32)],
        compiler_params=_params(("parallel",)),
        name="conv_sample",
    )(u_t, st_t, w["conv_ws"], w["conv_wu"], w["conv_b"], w["conv_ln_g"], w["conv_ln_b"])


def _attn_prompt_kernel(*refs, width, blk, has_bias, n_sub, n_q):
    if has_bias:
        q_ref, k_ref, v_ref, cq_ref, ck_ref, o_ref = refs
    else:
        q_ref, k_ref, v_ref, o_ref = refs
    i = pl.program_id(2)
    lane = lax.broadcasted_iota(jnp.int32, (1, LANES), 1)
    sub = blk // n_sub
    row = lax.broadcasted_iota(jnp.int32, (sub, blk), 0)
    col = lax.broadcasted_iota(jnp.int32, (sub, blk), 1)
    qs, lanes = [], []
    for e in range(2):
        if width == LANES:
            keep = (lane < DH) if e == 0 else (lane >= DH)
            qs.append(jnp.where(keep, q_ref[...], jnp.zeros((), q_ref.dtype)))
            lanes.append(slice(0, LANES))
        else:
            lanes.append(slice(e * LANES, (e + 1) * LANES))
            qs.append(q_ref[:, lanes[e]])

    def block(j, carry, masked):
        start = j * blk
        v = v_ref[pl.ds(start, blk), :]
        out = []
        for e in range(2):
            k = k_ref[pl.ds(start, blk), lanes[e]]
            for sb in range(n_sub):
                rows = slice(sb * sub, (sb + 1) * sub)
                m, l, acc = carry[e * n_sub + sb]
                z = lax.dot_general(qs[e][rows, :], k, _NT, preferred_element_type=F32)
                if has_bias:
                    z = z - ck_ref[j, e:e + 1, :] * LOG2E
                if masked:
                    z = jnp.where(col <= row + sb * sub, z, NEG_INF)
                m_new = jnp.maximum(m, jnp.max(z, axis=1, keepdims=True))
                shift = m_new
                if has_bias:
                    cq = cq_ref[rows, e:e + 1] * LOG2E
                    shift = (m_new + cq) - cq
                alpha = jnp.exp2(m - m_new)
                p = jnp.exp2(z - shift)
                l = alpha * l + jnp.sum(p, axis=1, keepdims=True)
                acc = alpha * acc + _mm(p.astype(BF16), v)
                out.append((m_new, l, acc))
        return tuple(out)

    init = (jnp.full((sub, 1), NEG_INF, F32), jnp.zeros((sub, 1), F32), jnp.zeros((sub, LANES), F32))
    for iv in range(n_q):
        @pl.when(i == iv)
        def _(iv=iv):
            carry = (init,) * (2 * n_sub)
            for j in range(iv):
                carry = block(j, carry, False)
            res = block(iv, carry, True)
            for sb in range(n_sub):
                (_, l0, acc0), (_, l1, acc1) = res[sb], res[n_sub + sb]
                o_ref[sb * sub:(sb + 1) * sub, :] = jnp.where(lane < DH, acc0 / l0, acc1 / l1).astype(o_ref.dtype)


def _attn_prompt(q, k, v, bias, n_seq, seq_len, width, blk):
    n_pairs = N_HEADS // 2
    nq = seq_len // blk
    in_specs = [pl.BlockSpec((blk, width), lambda b, p, i: (b * nq + i, p)),
                pl.BlockSpec((seq_len, width), lambda b, p, i: (b, p)),
                pl.BlockSpec((seq_len, LANES), lambda b, p, i: (b, p))]
    args = [q, k, v]
    if bias is not None:
        in_specs += [pl.BlockSpec((None, None, blk, 2), lambda b, p, i: (b, p, i, 0)),
                     pl.BlockSpec((None, None, nq, 2, blk), lambda b, p, i: (b, p, 0, 0, 0))]
        args += list(bias)
    kern = functools.partial(_attn_prompt_kernel, width=width, blk=blk, has_bias=bias is not None, n_sub=1,
                             n_q=nq)
    return pl.pallas_call(
        kern, grid=(n_seq, n_pairs, nq), in_specs=in_specs,
        out_specs=pl.BlockSpec((blk, LANES), lambda b, p, i: (b * nq + i, p)),
        out_shape=jax.ShapeDtypeStruct((n_seq * seq_len, n_pairs * LANES), BF16),
        compiler_params=_params(("parallel", "parallel", "arbitrary")),
        name="attn_prompt_fox" if bias is not None else "attn_prompt_mla",
    )(*args)


def _softmax_step(m, l, acc, z, pv, crow=None):
    m_new = jnp.maximum(m, jnp.max(z, axis=1, keepdims=True))
    shift = m_new if crow is None else (m_new + crow) - crow
    alpha = jnp.exp(m - m_new)
    p = jnp.exp(z - shift)
    return m_new, alpha * l + jnp.sum(p, axis=1, keepdims=True), alpha * acc + pv(p)


def _chunk_copies(pt_ref, caches, bufs, sems, layer, b, c, slot, n_pg, n_chunks, read_table):
    first = (n_chunks - 1 - c) * n_pg
    out = []
    for i in range(n_pg):
        page = pt_ref[b, first + i] if read_table else 0
        for a, (cache, buf) in enumerate(zip(caches, bufs)):
            out.append(pltpu.make_async_copy(cache.at[layer, page], buf.at[slot, i], sems.at[slot, a]))
    return out


def _page_pipeline(pt_ref, caches, bufs, sems, layer, n_pg, n_chunks):
    b = pl.program_id(0)
    c = pl.program_id(1)
    n_b = pl.num_programs(0)
    step = b * n_chunks + c
    slot = step % 2
    args = (pt_ref, caches, bufs, sems, layer)

    @pl.when(step == 0)
    def _():
        for cp in _chunk_copies(*args, b, c, slot, n_pg, n_chunks, True):
            cp.start()

    wrap = c == n_chunks - 1
    c_next = jnp.where(wrap, 0, c + 1)
    b_next = jnp.where(wrap, jnp.where(b == n_b - 1, 0, b + 1), b)
    for cp in _chunk_copies(*args, b_next, c_next, 1 - slot, n_pg, n_chunks, True):
        cp.start()

    for cp in _chunk_copies(*args, b, c, slot, n_pg, n_chunks, False):
        cp.wait()

    def finish():
        @pl.when(step == n_b * n_chunks - 1)
        def _():
            for cp in _chunk_copies(*args, b_next, c_next, 1 - slot, n_pg, n_chunks, False):
                cp.wait()

    return slot, finish


def _fox_dec_kernel(pt_ref, q_ref, kn_ref, vn_ref, cc_ref, ck_hbm, cv_hbm, cf_hbm, o_ref,
                    m_ref, l_ref, acc_ref, tot_ref, kbuf, vbuf, fbuf, sems, *, layer, n_pg, n_chunks, n_new):
    slot, finish = _page_pipeline(pt_ref, (ck_hbm, cv_hbm, cf_hbm), (kbuf, vbuf, fbuf), sems,
                                  layer, n_pg, n_chunks)
    k_pages = [kbuf.at[slot, i] for i in range(n_pg)]
    v_pages = [vbuf.at[slot, i] for i in range(n_pg)]
    f_pages = [fbuf.at[slot, i] for i in range(n_pg)]
    c = pl.program_id(1)
    hd = N_HEADS * DH
    n_rows = n_new * N_HEADS
    lane = lax.broadcasted_iota(jnp.int32, (N_HEADS, hd), 1)
    sub = lax.broadcasted_iota(jnp.int32, (N_HEADS, hd), 0)
    hmask = (lane // DH) == sub
    q_new = q_ref[...]
    q_bd = jnp.concatenate(
        [jnp.where(hmask, jnp.broadcast_to(q_new[t:t + 1, :], (N_HEADS, hd)), 0.0) for t in range(n_new)],
        axis=0)
    ccol = cc_ref[...]
    trow = lax.broadcasted_iota(jnp.int32, (n_rows, 1), 0) // N_HEADS

    @pl.when(c == 0)
    def _():
        tot_ref[...] = jnp.zeros_like(tot_ref)
        m = l = acc = None
        for s_idx in range(n_new):
            k_row = kn_ref[s_idx:s_idx + 1, :]
            v_row = vn_ref[s_idx:s_idx + 1, :]
            c_key = jnp.concatenate([ccol[s_idx * N_HEADS:(s_idx + 1) * N_HEADS, :]] * n_new, axis=0)
            z = jnp.sum(q_bd * k_row, axis=1, keepdims=True) - c_key
            if s_idx == 0:
                m, l, acc = z, jnp.ones_like(z), jnp.broadcast_to(v_row, (n_rows, hd))
            else:
                z = jnp.where(trow >= s_idx, z, NEG_INF)
                m, l, acc = _softmax_step(m, l, acc, z, lambda p, v_row=v_row: p * v_row, ccol)
        m_ref[...] = m
        l_ref[...] = l
        acc_ref[...] = acc

    lf = jnp.concatenate([f_pages[i][...] for i in range(n_pg)], axis=0)
    lane_pg = lax.broadcasted_iota(jnp.int32, lf.shape, 1)
    inc = lf
    d = 1
    while d < LANES:
        inc = inc + jnp.where(lane_pg < LANES - d, pltpu.roll(inc, LANES - d, axis=1), 0.0)
        d *= 2
    exc = inc - lf
    run = tot_ref[...]
    sufs = [None] * n_pg
    for i in reversed(range(n_pg)):
        rows = slice(i * N_HEADS, (i + 1) * N_HEADS)
        sufs[i] = exc[rows, :] + run
        run = run + inc[rows, 0:1]
    tot_ref[...] = run
    suffix = jnp.concatenate(sufs, axis=1)

    qb = q_bd.astype(BF16)
    z = jnp.concatenate([_mm(qb, k_pages[i][...].astype(BF16)) for i in range(n_pg)], axis=1)
    z = z + jnp.concatenate([suffix] * n_new, axis=0)

    def pv(p):
        out = None
        for i in range(n_pg):
            part = lax.dot_general(p[:, i * LANES:(i + 1) * LANES].astype(BF16), v_pages[i][...].astype(BF16),
                                   _NT, preferred_element_type=F32)
            out = part if out is None else out + part
        return out

    m, l, acc = _softmax_step(m_ref[...], l_ref[...], acc_ref[...], z, pv, ccol)
    m_ref[...] = m
    l_ref[...] = l
    acc_ref[...] = acc

    @pl.when(c == n_chunks - 1)
    def _():
        o = acc / l
        for t in range(n_new):
            o_ref[t:t + 1, :] = jnp.sum(jnp.where(hmask, o[t * N_HEADS:(t + 1) * N_HEADS, :], 0.0),
                                        axis=0, keepdims=True)

    finish()


def _small_spec(shape):
    return pl.BlockSpec((None,) + shape, lambda b, c, pt: (b,) + (0,) * len(shape))


def _fox_dec(page_table, q, k_new, v_new, ccol, cache_k, cache_v, cache_f, layer, n_pg):
    n_b, n_new, hd = q.shape
    n_pages = page_table.shape[1]
    n_chunks = n_pages // n_pg
    page = cache_k.shape[3]
    n_rows = n_new * N_HEADS
    hbm = pl.BlockSpec(memory_space=pl.ANY)
    in_specs = [_small_spec((n_new, hd)), _small_spec((n_new, hd)), _small_spec((n_new, hd)),
                _small_spec((n_rows, 1)), hbm, hbm, hbm]
    kern = functools.partial(_fox_dec_kernel, layer=layer, n_pg=n_pg, n_chunks=n_chunks, n_new=n_new)
    return pl.pallas_call(
        kern,
        grid_spec=pltpu.PrefetchScalarGridSpec(
            num_scalar_prefetch=1, grid=(n_b, n_chunks), in_specs=in_specs,
            out_specs=pl.BlockSpec((None, n_new, hd), lambda b, c, pt: (b, 0, 0)),
            scratch_shapes=[pltpu.VMEM((n_rows, 1), F32), pltpu.VMEM((n_rows, 1), F32),
                            pltpu.VMEM((n_rows, hd), F32), pltpu.VMEM((N_HEADS, 1), F32),
                            pltpu.VMEM((2, n_pg, hd, page), F32), pltpu.VMEM((2, n_pg, hd, page), F32),
                            pltpu.VMEM((2, n_pg, N_HEADS, page), F32), pltpu.SemaphoreType.DMA((2, 3))]),
        out_shape=jax.ShapeDtypeStruct((n_b, n_new, hd), F32),
        compiler_params=_params(("arbitrary", "arbitrary")),
        name="fox_decode",
    )(page_table, q, k_new, v_new, ccol, cache_k, cache_v, cache_f)


def _mla_dec_kernel(pt_ref, qa_ref, qr_ref, ln_ref, rn_ref, cl_hbm, cr_hbm, o_ref,
                    m_ref, l_ref, acc_ref, lbuf, rbuf, sems, *, layer, n_pg, n_chunks, n_new):
    slot, finish = _page_pipeline(pt_ref, (cl_hbm, cr_hbm), (lbuf, rbuf), sems, layer, n_pg, n_chunks)
    l_pages = [lbuf.at[slot, i] for i in range(n_pg)]
    r_pages = [rbuf.at[slot, i] for i in range(n_pg)]
    c = pl.program_id(1)
    n_rows = n_new * N_HEADS
    qa = qa_ref[...]
    qr = qr_ref[...]
    trow = lax.broadcasted_iota(jnp.int32, (n_rows, 1), 0) // N_HEADS

    @pl.when(c == 0)
    def _():
        m = l = acc = None
        for s_idx in range(n_new):
            lat_row = ln_ref[s_idx:s_idx + 1, :]
            kr_row = rn_ref[s_idx:s_idx + 1, :]
            sc = (jnp.sum(qa * lat_row, axis=1, keepdims=True)
                  + jnp.sum(qr * kr_row, axis=1, keepdims=True))
            if s_idx == 0:
                m, l, acc = sc, jnp.ones_like(sc), jnp.broadcast_to(lat_row, (n_rows, KV_LORA))
            else:
                sc = jnp.where(trow >= s_idx, sc, NEG_INF)
                m, l, acc = _softmax_step(m, l, acc, sc, lambda p, lat_row=lat_row: p * lat_row)
        m_ref[...] = m
        l_ref[...] = l
        acc_ref[...] = acc

    qab = qa.astype(BF16)
    qrb = qr.astype(BF16)
    lat = [l_pages[i][...].astype(BF16) for i in range(n_pg)]
    z = jnp.concatenate([lax.dot_general(qab, lat[i], _NT, preferred_element_type=F32)
                         + _mm(qrb, r_pages[i][...].astype(BF16)) for i in range(n_pg)], axis=1)

    def pv(p):
        out = None
        for i in range(n_pg):
            part = _mm(p[:, i * LANES:(i + 1) * LANES].astype(BF16), lat[i])
            out = part if out is None else out + part
        return out

    m, l, acc = _softmax_step(m_ref[...], l_ref[...], acc_ref[...], z, pv)
    m_ref[...] = m
    l_ref[...] = l
    acc_ref[...] = acc

    @pl.when(c == n_chunks - 1)
    def _():
        o_ref[...] = acc / l

    finish()


def _mla_dec(page_table, qabs, qrope, lat_new, kr_new, cache_lat, cache_kr, layer, n_pg):
    n_b, n_rows, _ = qabs.shape
    n_new = lat_new.shape[1]
    n_pages = page_table.shape[1]
    n_chunks = n_pages // n_pg
    page = cache_lat.shape[2]
    hbm = pl.BlockSpec(memory_space=pl.ANY)
    in_specs = [_small_spec((n_rows, KV_LORA)), _small_spec((n_rows, ROPE)), _small_spec((n_new, KV_LORA)),
                _small_spec((n_new, ROPE)), hbm, hbm]
    kern = functools.partial(_mla_dec_kernel, layer=layer, n_pg=n_pg, n_chunks=n_chunks, n_new=n_new)
    return pl.pallas_call(
        kern,
        grid_spec=pltpu.PrefetchScalarGridSpec(
            num_scalar_prefetch=1, grid=(n_b, n_chunks), in_specs=in_specs,
            out_specs=pl.BlockSpec((None, n_rows, KV_LORA), lambda b, c, pt: (b, 0, 0)),
            scratch_shapes=[pltpu.VMEM((n_rows, 1), F32), pltpu.VMEM((n_rows, 1), F32),
                            pltpu.VMEM((n_rows, KV_LORA), F32),
                            pltpu.VMEM((2, n_pg, page, KV_LORA), F32), pltpu.VMEM((2, n_pg, ROPE, page), F32),
                            pltpu.SemaphoreType.DMA((2, 2))]),
        out_shape=jax.ShapeDtypeStruct((n_b, n_rows, KV_LORA), F32),
        compiler_params=_params(("arbitrary", "arbitrary")),
        name="mla_decode",
    )(page_table, qabs, qrope, lat_new, kr_new, cache_lat, cache_kr)


def _merge_kernel(*refs, latent_out, alpha):
    (x_ref, shift_ref, scale_ref, gate_ref, a_ref, of_ref, om_ref, wg_ref, wc_ref, wf_ref) = refs[:10]
    refs = refs[10:]
    if latent_out:
        bduv_ref = refs[0]
        refs = refs[1:]
    wm_ref, wo_ref, g_ref, b_ref, o_ref = refs
    x = x_ref[...]
    h = (_ln(x) * (1.0 + scale_ref[...]) + shift_ref[...]).astype(BF16)
    y_conv = _mm(a_ref[...].astype(BF16), wc_ref[...])
    y_fox = _mm(of_ref[...].astype(BF16), wf_ref[...])
    om = om_ref[...].astype(BF16)
    if latent_out:
        om = _mm(om, bduv_ref[...]).astype(BF16)
    y_mla = _mm(om, wm_ref[...])
    merged = (_sigmoid(_mm(h, wg_ref[:, 0:D_MODEL])) * y_conv
              + _sigmoid(_mm(h, wg_ref[:, D_MODEL:2 * D_MODEL])) * y_fox
              + _sigmoid(_mm(h, wg_ref[:, 2 * D_MODEL:3 * D_MODEL])) * y_mla)
    mix = _mm(merged.astype(BF16), wo_ref[...])
    o_ref[...] = _ln(alpha * x + gate_ref[...] * mix) * g_ref[...] + b_ref[...]


def _merge(x, mod_arr, mod_specs, a, o_fox, o_mla, w, *, latent_out, tm, alpha):
    rows = x.shape[0]
    row_spec = lambda width: pl.BlockSpec((tm, width), lambda i: (i, 0))
    weights = [w["w_gate"], w["conv_w_out"], w["fox_w_o"]] + ([w["bd_uv"]] if latent_out else []) + [
        w["mla_w_o"], w["w_out"], w["ln1_g"], w["ln1_b"]]
    in_specs = ([row_spec(D_MODEL)] + list(mod_specs) + [row_spec(a.shape[1]), row_spec(o_fox.shape[1]),
                                                         row_spec(o_mla.shape[1])]
                + [_const_spec(t.shape) for t in weights])
    kern = functools.partial(_merge_kernel, latent_out=latent_out, alpha=alpha)
    return pl.pallas_call(
        kern, grid=(rows // tm,), in_specs=in_specs, out_specs=row_spec(D_MODEL),
        out_shape=jax.ShapeDtypeStruct((rows, D_MODEL), F32),
        compiler_params=_params(("parallel",)),
        name="merge",
    )(x, mod_arr, mod_arr, mod_arr, a, o_fox, o_mla, *weights)


def _ffn_kernel(x_ref, shift_ref, scale_ref, gate_ref, wu_ref, wd_ref, g_ref, b_ref, o_ref, *, alpha, chunk):
    x = x_ref[...]
    h = (_ln(x) * (1.0 + scale_ref[...]) + shift_ref[...]).astype(BF16)
    ff = jnp.zeros(x.shape, F32)
    for s in range(0, D_FF, chunk):
        t = jnp.maximum(_mm(h, wu_ref[:, s:s + chunk]), 0.0)
        ff = ff + _mm((t * t).astype(BF16), wd_ref[s:s + chunk, :])
    o_ref[...] = _ln(alpha * x + gate_ref[...] * ff) * g_ref[...] + b_ref[...]


def _ffn(x, mod_arr, mod_specs, w, *, tm, alpha):
    rows = x.shape[0]
    row_spec = pl.BlockSpec((tm, D_MODEL), lambda i: (i, 0))
    weights = [w["w_up"], w["w_down"], w["ln2_g"], w["ln2_b"]]
    kern = functools.partial(_ffn_kernel, alpha=alpha, chunk=1024)
    return pl.pallas_call(
        kern, grid=(rows // tm,), in_specs=[row_spec] + list(mod_specs) + [_const_spec(t.shape) for t in weights],
        out_specs=row_spec, out_shape=jax.ShapeDtypeStruct((rows, D_MODEL), F32),
        compiler_params=_params(("parallel",)),
        name="ffn",
    )(x, mod_arr, mod_arr, mod_arr, *weights)


def _swap_halves(w):
    half = w.shape[-1] // 2
    return jnp.concatenate([w[..., half:], w[..., :half]], axis=-1)


def _block_diag(blocks):
    n, r, c = blocks.shape
    eye = jnp.eye(n, dtype=blocks.dtype)
    return (eye[:, None, :, None] * blocks[:, :, None, :]).reshape(n * r, n * c)


def _layer_weights(p, n_new):
    w_in = p["w_in"]
    hd = N_HEADS * DH
    o_f = 2 * C_CONV + 3 * hd
    o_ql = o_f + N_HEADS
    o_kv = o_ql + Q_LORA
    o_kr = o_kv + KV_LORA
    o_g = o_kr + ROPE
    w_f = w_in[:, o_f:o_ql]
    w_kr = w_in[:, o_kr:o_g]
    w_krr = _swap_halves(w_kr)
    zeros = lambda n: jnp.zeros((D_MODEL, n), F32)
    w = {}
    w["w_main"] = w_in[:, :o_f].astype(BF16)
    w["w_ql"] = w_in[:, o_ql:o_kv].astype(BF16)
    w["w_kv"] = w_in[:, o_kv:o_kr].astype(BF16)
    w["w_s"] = jnp.concatenate([w_kr, w_krr, w_f, zeros(LANES - 2 * ROPE - N_HEADS)], axis=1).astype(BF16)
    w["b_f"] = jnp.zeros((1, LANES), F32).at[0, 2 * ROPE:2 * ROPE + N_HEADS].set(p["b_f"])
    w["g_q"] = p["mla_q_norm_g"].reshape(1, Q_LORA)
    w["g_kv"] = p["mla_kv_norm_g"].reshape(1, KV_LORA)
    w["w_gate"] = w_in[:, o_g:].astype(BF16)
    w["w_kr"] = jnp.concatenate([zeros(DH), w_kr, zeros(HEAD_PAD - DH - ROPE)], axis=1).astype(BF16)
    w["w_krr"] = jnp.concatenate([zeros(DH), w_krr, zeros(HEAD_PAD - DH - ROPE)], axis=1).astype(BF16)
    w_qb = p["mla_w_qb"].reshape(Q_LORA, N_HEADS, DH + ROPE)
    qz = lambda n: jnp.zeros((Q_LORA, N_HEADS, n), F32)
    w["w_qa"] = jnp.concatenate([w_qb, qz(HEAD_PAD - DH - ROPE)], axis=2).reshape(Q_LORA, -1).astype(BF16)
    w["w_qb"] = jnp.concatenate([qz(DH), _swap_halves(w_qb[..., DH:]), qz(HEAD_PAD - DH - ROPE)],
                                axis=2).reshape(Q_LORA, -1).astype(BF16)
    w_uk = p["mla_w_uk"]
    w["w_ka"] = jnp.concatenate([w_uk, jnp.zeros((KV_LORA, N_HEADS, HEAD_PAD - DH), F32)],
                                axis=2).reshape(KV_LORA, -1).astype(BF16)
    w["w_uv"] = p["mla_w_uv"].reshape(KV_LORA, hd).astype(BF16)
    w["w_qn"] = w_qb[..., :DH].reshape(Q_LORA, hd).astype(BF16)
    w["w_qr"] = w_qb[..., DH:].reshape(Q_LORA, N_HEADS * ROPE).astype(BF16)
    w["w_qrr"] = _swap_halves(w_qb[..., DH:]).reshape(Q_LORA, N_HEADS * ROPE).astype(BF16)
    w["bd_uk"] = _block_diag(jnp.transpose(w_uk, (1, 2, 0))).astype(BF16)
    w["bd_uv"] = _block_diag(jnp.transpose(p["mla_w_uv"], (1, 0, 2))).astype(BF16)
    cw = p["conv_w"]
    w["conv_w"] = jnp.concatenate([cw, jnp.zeros((_HALO - CONV_WIDTH, C_CONV), F32)], axis=0)
    n_st = CONV_WIDTH - 1
    idx = jnp.arange(n_st + n_new)[None, :] - jnp.arange(n_new)[:, None]
    taps = jnp.where(((idx >= 0) & (idx < CONV_WIDTH))[..., None], cw[jnp.clip(idx, 0, CONV_WIDTH - 1)], 0.0)
    w["conv_ws"] = taps[:, :n_st]
    w["conv_wu"] = taps[:, n_st:]
    for name in ("conv_b", "conv_ln_g", "conv_ln_b"):
        w[name] = p[name].reshape(1, C_CONV)
    for name in ("conv_w_out", "fox_w_o", "mla_w_o", "w_out", "w_up", "w_down"):
        w[name] = p[name].astype(BF16)
    for name in ("ln1_g", "ln1_b", "ln2_g", "ln2_b"):
        w[name] = p[name].reshape(1, D_MODEL)
    return w


def _rope_tables(pos, reps):
    half = ROPE // 2
    inv_freq = ROPE_BASE ** (-jnp.arange(half, dtype=F32) / half)
    ang = pos.astype(F32)[:, None] * inv_freq
    cos, sin = jnp.cos(ang), jnp.sin(ang)
    cos32 = jnp.concatenate([cos, cos], axis=1)
    sin32 = jnp.concatenate([-sin, sin], axis=1)
    n = pos.shape[0]
    tabs = {"cos32": cos32, "sin32": sin32,
            "cos128": jnp.concatenate([jnp.ones((n, DH), F32), cos32, jnp.zeros((n, HEAD_PAD - DH - ROPE), F32)], 1),
            "sin128": jnp.concatenate([jnp.zeros((n, DH), F32), sin32, jnp.zeros((n, HEAD_PAD - DH - ROPE), F32)], 1),
            "cos256": jnp.tile(cos32, (1, N_HEADS)), "sin256": jnp.tile(sin32, (1, N_HEADS))}
    if reps > 1:
        tabs = {k: jnp.tile(v, (reps, 1)) for k, v in tabs.items()}
    return tabs


def _mod_specs_seq(cols, tiles_per_seq):
    return [pl.BlockSpec((None, 1, D_MODEL), functools.partial(
        lambda i, c: (i // tiles_per_seq, 0, c), c=c)) for c in cols]


def _mod_specs_rows(cols, tm):
    return [pl.BlockSpec((tm, D_MODEL), functools.partial(lambda i, c: (i, c), c=c)) for c in cols]


def kernel(x_prompt, x_sample, cache_fox_k, cache_fox_v, cache_fox_logf, cache_mla_latent, cache_mla_krope,
           state_conv, page_table, c_prompt, c_sample, w_ada, b_ada, w_in, b_f, conv_w, conv_b, conv_ln_g,
           conv_ln_b, conv_w_out, fox_w_o, mla_q_norm_g, mla_w_qb, mla_kv_norm_g, mla_w_uk, mla_w_uv, mla_w_o,
           w_out, ln1_g, ln1_b, w_up, w_down, ln2_g, ln2_b):
    n_layers = w_in.shape[0]
    n_b, seq_len, _ = x_prompt.shape
    n_s, n_new, _ = x_sample.shape
    page = cache_fox_k.shape[2]
    n_pages = page_table.shape[1]
    n_past = n_pages * page
    alpha = float((2 * n_layers) ** 0.25)
    hd = N_HEADS * DH
    n_pairs = N_HEADS // 2

    tm_p = min(512, seq_len)
    rows_s = n_s * n_new
    tm_s = min(256, rows_s)
    blk = min(512, seq_len)
    n_pg_fox = min(32, n_pages)
    n_pg_mla = min(64, n_pages)
    nb_conv = min(8, n_s)

    n_pool = cache_fox_k.shape[1]
    ck = jnp.transpose(cache_fox_k, (0, 1, 3, 4, 2)).reshape(n_layers, n_pool, hd, page)
    cv = jnp.transpose(cache_fox_v, (0, 1, 3, 4, 2)).reshape(n_layers, n_pool, hd, page)
    cf = jnp.transpose(cache_fox_logf, (0, 1, 3, 2))
    ckr = jnp.transpose(cache_mla_krope, (0, 1, 3, 2))

    m_all = n_b + n_s
    m_pad = -(-m_all // 8) * 8
    c_all = jnp.concatenate([c_prompt, c_sample, jnp.zeros((m_pad - m_all, D_MODEL), F32)], axis=0)
    mod_all = _ada(c_all, w_ada, b_ada)

    tabs_p = _rope_tables(jnp.arange(seq_len, dtype=jnp.int32), 1)
    tabs_s = _rope_tables(n_past + jnp.arange(n_new, dtype=jnp.int32), n_s)

    params = dict(w_in=w_in, b_f=b_f, conv_w=conv_w, conv_b=conv_b, conv_ln_g=conv_ln_g, conv_ln_b=conv_ln_b,
                  conv_w_out=conv_w_out, fox_w_o=fox_w_o, mla_q_norm_g=mla_q_norm_g, mla_w_qb=mla_w_qb,
                  mla_kv_norm_g=mla_kv_norm_g, mla_w_uk=mla_w_uk, mla_w_uv=mla_w_uv, mla_w_o=mla_w_o,
                  w_out=w_out, ln1_g=ln1_g, ln1_b=ln1_b, w_up=w_up, w_down=w_down, ln2_g=ln2_g, ln2_b=ln2_b)

    xp = x_prompt.reshape(n_b * seq_len, D_MODEL)
    xs = x_sample.reshape(rows_s, D_MODEL)
    rows_p, rows_smp = [], []
    tps = seq_len // tm_p
    for l in range(n_layers):
        w = _layer_weights({k: v[l] for k, v in params.items()}, n_new)
        mod_p = mod_all[l, :n_b].reshape(n_b, 1, 6 * D_MODEL)
        mod_s = jnp.repeat(mod_all[l, n_b:n_b + n_s], n_new, axis=0)

        (u, kf, vf, logf, cs, lat, kr, qf, kb, vb, mq, mk, mv) = _inproj(
            xp, mod_p, _mod_specs_seq((0, 1), tps), w, tabs_p, prompt=True, seg=seq_len, tm=tm_p)
        a_p, st_p = _conv_prompt(u, w, n_b, seq_len, tm_p)
        c4 = cs.reshape(n_b, seq_len, n_pairs, 2)
        cq = jnp.transpose(c4, (0, 2, 1, 3))
        ckk = jnp.transpose(c4.reshape(n_b, seq_len // blk, blk, n_pairs, 2), (0, 3, 1, 4, 2))
        o_fox = _attn_prompt(qf, kb, vb, (cq, ckk), n_b, seq_len, LANES, blk)
        o_mla = _attn_prompt(mq, mk, mv, None, n_b, seq_len, 2 * HEAD_PAD, blk)
        x1 = _merge(xp, mod_p, _mod_specs_seq((0, 1, 2), tps), a_p, o_fox, o_mla, w,
                    latent_out=False, tm=tm_p, alpha=alpha)
        xp = _ffn(x1, mod_p, _mod_specs_seq((3, 4, 5), tps), w, tm=tm_p, alpha=alpha)
        rows_p.append((kf.reshape(n_b, seq_len, N_HEADS, DH), vf.reshape(n_b, seq_len, N_HEADS, DH),
                       logf.reshape(n_b, seq_len, N_HEADS), lat.reshape(n_b, seq_len, KV_LORA),
                       kr.reshape(n_b, seq_len, ROPE), st_p))

        (u, kf, vf, logf, cs, lat, kr, qf, qabs, qrope) = _inproj(
            xs, mod_s, _mod_specs_rows((0, 1), tm_s), w, tabs_s, prompt=False, seg=n_new, tm=tm_s)
        u_t = jnp.transpose(u.reshape(n_s, n_new, C_CONV), (1, 0, 2))
        st_t = jnp.transpose(state_conv[l], (1, 0, 2))
        a_t, nst_t = _conv_sample(u_t, st_t, w, nb_conv)
        a_s = jnp.transpose(a_t, (1, 0, 2)).reshape(rows_s, C_CONV)
        st_s = jnp.transpose(nst_t, (1, 0, 2))
        o_fox = _fox_dec(page_table, qf.reshape(n_s, n_new, hd), kf.reshape(n_s, n_new, hd),
                         vf.reshape(n_s, n_new, hd), cs.reshape(n_s, n_new * N_HEADS, 1),
                         ck, cv, cf, l, n_pg_fox)
        o_lat = _mla_dec(page_table, qabs.reshape(n_s, n_new * N_HEADS, KV_LORA),
                         qrope.reshape(n_s, n_new * N_HEADS, ROPE), lat.reshape(n_s, n_new, KV_LORA),
                         kr.reshape(n_s, n_new, ROPE), cache_mla_latent, ckr, l, n_pg_mla)
        x1 = _merge(xs, mod_s, _mod_specs_rows((0, 1, 2), tm_s), a_s, o_fox.reshape(rows_s, hd),
                    o_lat.reshape(rows_s, N_HEADS * KV_LORA), w, latent_out=True, tm=tm_s, alpha=alpha)
        xs = _ffn(x1, mod_s, _mod_specs_rows((3, 4, 5), tm_s), w, tm=tm_s, alpha=alpha)
        rows_smp.append((kf.reshape(n_s, n_new, N_HEADS, DH), vf.reshape(n_s, n_new, N_HEADS, DH),
                         logf.reshape(n_s, n_new, N_HEADS), lat.reshape(n_s, n_new, KV_LORA),
                         kr.reshape(n_s, n_new, ROPE), st_s))

    def stk(rows, i):
        return jnp.stack([r[i] for r in rows])

    return ((xp.reshape(n_b, seq_len, D_MODEL), xs.reshape(n_s, n_new, D_MODEL))
            + tuple(stk(rows_p, i) for i in range(6)) + tuple(stk(rows_smp, i) for i in range(6)))
```
